```python
import math
import jax, jax.numpy as jnp
from jax import lax
import numpy as np

D_MODEL = 2048
BATCH = 8
SEQ = 4096
DEPTH = 4

F32 = jnp.float32
LN_EPS = 1e-5
NORM_EPS = 1e-6

N_BRANCH = 3
BRANCH_WIDTH = D_MODEL // 2

M_HEADS = 4
M_DV = BRANCH_WIDTH // M_HEADS
M_DQK = M_DV // 2
M_CHUNK = 64

S_DINNER = BRANCH_WIDTH
S_HEADDIM = 64
S_HEADS = S_DINNER // S_HEADDIM
S_GROUPS = 2
S_HPG = S_HEADS // S_GROUPS
S_DSTATE = 128
S_CONV = 4
S_CHUNK = 128
S_CONV_CH = S_DINNER + 2 * S_GROUPS * S_DSTATE

A_HEADDIM = 64
A_QHEADS = BRANCH_WIDTH // A_HEADDIM
A_KVHEADS = 4
A_REP = A_QHEADS // A_KVHEADS
A_WINDOW = 128
A_BLOCK = 128
ROPE_THETA = 10000.0

P_HEADS = 8
P_NKEYS = 128
P_EXPERTS = P_NKEYS * P_NKEYS
P_DKEY = 256
P_TOPK = 16
P_CHUNK = 128

M_COLS = 2 * M_HEADS * M_DQK + 2 * M_HEADS * M_DV + 2 * M_HEADS
S_COLS = S_DINNER + S_CONV_CH + S_HEADS
A_COLS = A_QHEADS * A_HEADDIM + 2 * A_KVHEADS * A_HEADDIM
G_COLS = N_BRANCH * D_MODEL
IN_COLS = M_COLS + S_COLS + A_COLS + G_COLS

kernel_name = "hybrid_mlstm_ssd_swa_peer_deepnorm"


def _in_split_points():
    sizes = [M_HEADS * M_DQK, M_HEADS * M_DQK, M_HEADS * M_DV, M_HEADS * M_DV, M_HEADS, M_HEADS,
             S_DINNER, S_CONV_CH, S_HEADS,
             A_QHEADS * A_HEADDIM, A_KVHEADS * A_HEADDIM, A_KVHEADS * A_HEADDIM,
             G_COLS]
    return [int(v) for v in np.cumsum(sizes)[:-1]]


def layer_norm(x, g, b):
    xf = x.astype(F32)
    mu = jnp.mean(xf, -1, keepdims=True)
    var = jnp.mean(jnp.square(xf - mu), -1, keepdims=True)
    return ((xf - mu) * lax.rsqrt(var + LN_EPS) * g.astype(F32) + b.astype(F32)).astype(x.dtype)


def mlstm_mixer(q, k, v, o_pre, i_pre, f_pre, norm_w):
    B_, S_ = q.shape[:2]
    nc = S_ // M_CHUNK

    def to_chunks(t):
        t = t.reshape((B_, nc, M_CHUNK) + t.shape[2:])
        return jnp.moveaxis(jnp.moveaxis(t, 3, 2), 1, 0)

    qc = to_chunks(q.astype(F32))
    kc = to_chunks(k.astype(F32) * (M_DQK ** -0.5))
    vc = to_chunks(v.astype(F32))
    li = to_chunks(i_pre.astype(F32))
    lf = to_chunks(jax.nn.log_sigmoid(f_pre.astype(F32)))
    causal = jnp.tril(jnp.ones((M_CHUNK, M_CHUNK), dtype=bool))

    def step(carry, inp):
        C, n, m = carry
        qb, kb, vb, lib, lfb = inp
        b = jnp.cumsum(lfb, axis=-1)
        g = b[..., -1]
        log_d = jnp.where(causal, b[..., :, None] - b[..., None, :] + lib[..., None, :], -jnp.inf)
        log_inter = b + m[..., None]
        m_t = jnp.maximum(jnp.max(log_d, -1), log_inter)
        w_intra = jnp.exp(log_d - m_t[..., None])
        w_inter = jnp.exp(log_inter - m_t)
        s = jnp.einsum('bhtd,bhsd->bhts', qb, kb) * w_intra
        num = jnp.einsum('bhts,bhsv->bhtv', s, vb) + w_inter[..., None] * jnp.einsum('bhtd,bhdv->bhtv', qb, C)
        den = jnp.sum(s, -1) + w_inter * jnp.einsum('bhtd,bhd->bht', qb, n)
        h = num / jnp.maximum(jnp.abs(den), jnp.exp(-m_t))[..., None]
        log_ws = g[..., None] - b + lib
        m_new = jnp.maximum(g + m, jnp.max(log_ws, -1))
        a_prev = jnp.exp(g + m - m_new)
        ws = jnp.exp(log_ws - m_new[..., None])
        C_new = a_prev[..., None, None] * C + jnp.einsum('bhsd,bhsv->bhdv', kb * ws[..., None], vb)
        n_new = a_prev[..., None] * n + jnp.einsum('bhs,bhsd->bhd', ws, kb)
        return (C_new, n_new, m_new), h

    init = (jnp.zeros((B_, M_HEADS, M_DQK, M_DV), F32),
            jnp.zeros((B_, M_HEADS, M_DQK), F32),
            jnp.zeros((B_, M_HEADS), F32))
    _, hs = lax.scan(step, init, (qc, kc, vc, li, lf))
    h = jnp.swapaxes(jnp.moveaxis(hs, 0, 1), 2, 3).reshape(B_, S_, M_HEADS, M_DV)
    mu = jnp.mean(h, -1, keepdims=True)
    var = jnp.mean(jnp.square(h - mu), -1, keepdims=True)
    h = ((h - mu) * lax.rsqrt(var + NORM_EPS)).reshape(B_, S_, M_HEADS * M_DV) * norm_w.astype(F32)
    return (jax.nn.sigmoid(o_pre.astype(F32)) * h).astype(q.dtype)


def causal_depthwise_conv(u, w, b):
    out = lax.conv_general_dilated(u, w.astype(u.dtype), window_strides=(1,), padding=[(S_CONV - 1, 0)],
                                   dimension_numbers=('NWC', 'WIO', 'NWC'),
                                   feature_group_count=u.shape[-1])
    return out + b.astype(u.dtype)


def ssd_mixer(z, xbc, dt_raw, conv_w, conv_b, dt_bias, a_log, d_skip, norm_w):
    B_, S_ = z.shape[:2]
    xbc = jax.nn.silu(causal_depthwise_conv(xbc, conv_w, conv_b)).astype(F32)
    xs, bm, cm = jnp.split(xbc, [S_DINNER, S_DINNER + S_GROUPS * S_DSTATE], axis=-1)
    xs = xs.reshape(B_, S_, S_GROUPS, S_HPG, S_HEADDIM)
    dt = jax.nn.softplus(dt_raw.astype(F32) + dt_bias.astype(F32)).reshape(B_, S_, S_GROUPS, S_HPG)
    a = -jnp.exp(a_log.astype(F32)).reshape(S_GROUPS, S_HPG)
    nc = S_ // S_CHUNK
    xc = (xs * dt[..., None]).reshape(B_, nc, S_CHUNK, S_GROUPS, S_HPG, S_HEADDIM)
    bc = bm.reshape(B_, nc, S_CHUNK, S_GROUPS, S_DSTATE)
    cc = cm.reshape(B_, nc, S_CHUNK, S_GROUPS, S_DSTATE)
    acum = jnp.cumsum((dt * a).reshape(B_, nc, S_CHUNK, S_GROUPS, S_HPG), axis=2)
    causal = jnp.tril(jnp.ones((S_CHUNK, S_CHUNK), dtype=bool))[None, None, :, :, None, None]
    decay = jnp.exp(jnp.where(causal, acum[:, :, :, None] - acum[:, :, None, :], -jnp.inf))
    cb = jnp.einsum('bclgn,bcsgn->bclsg', cc, bc)
    y_diag = jnp.einsum('bclsgh,bcsghp->bclghp', cb[..., None] * decay, xc)
    decay_states = jnp.exp(acum[:, :, -1:] - acum)
    states = jnp.einsum('bclgn,bclghp->bcghpn', bc, decay_states[..., None] * xc)
    chunk_decay = jnp.exp(acum[:, :, -1])

    def step(hstate, inp):
        st, dec = inp
        return dec[..., None, None] * hstate + st, hstate

    h0 = jnp.zeros((B_, S_GROUPS, S_HPG, S_HEADDIM, S_DSTATE), F32)
    _, prev = lax.scan(step, h0, (jnp.moveaxis(states, 1, 0), jnp.moveaxis(chunk_decay, 1, 0)))
    prev = jnp.moveaxis(prev, 0, 1)
    y_off = jnp.einsum('bclgn,bcghpn->bclghp', cc, prev) * jnp.exp(acum)[..., None]
    y = (y_diag + y_off).reshape(B_, S_, S_GROUPS, S_HPG, S_HEADDIM) \
        + d_skip.astype(F32).reshape(S_GROUPS, S_HPG)[..., None] * xs
    y = y.reshape(B_, S_, S_DINNER) * jax.nn.silu(z.astype(F32))
    yg = y.reshape(B_, S_, S_GROUPS, S_DINNER // S_GROUPS)
    yg = yg * lax.rsqrt(jnp.mean(jnp.square(yg), -1, keepdims=True) + NORM_EPS)
    return (yg.reshape(B_, S_, S_DINNER) * norm_w.astype(F32)).astype(z.dtype)


def rope(x, pos):
    half = x.shape[-1] // 2
    freqs = ROPE_THETA ** (-jnp.arange(half, dtype=F32) / half)
    ang = pos.astype(F32)[:, None] * freqs[None, :]
    cos = jnp.cos(ang)[None, :, None, :]
    sin = jnp.sin(ang)[None, :, None, :]
    x1, x2 = x[..., :half], x[..., half:]
    return jnp.concatenate([x1 * cos - x2 * sin, x2 * cos + x1 * sin], axis=-1)


def swa_mixer(q, k, v, sinks):
    B_, S_ = q.shape[:2]
    pos = jnp.arange(S_)
    q = rope(q.astype(F32), pos)
    k = rope(k.astype(F32), pos)
    v = v.astype(F32)
    nb = S_ // A_BLOCK
    qb = q.reshape(B_, nb, A_BLOCK, A_KVHEADS, A_REP, A_HEADDIM)

    def with_prev(t):
        t = t.reshape(B_, nb, A_BLOCK, A_KVHEADS, A_HEADDIM)
        prev = jnp.pad(t[:, :-1], ((0, 0), (1, 0), (0, 0), (0, 0), (0, 0)))
        return jnp.concatenate([prev, t], axis=2)

    kk, vv = with_prev(k), with_prev(v)
    s = jnp.einsum('bnqgrd,bnkgd->bngrqk', qb, kk) * (A_HEADDIM ** -0.5)
    qi = jnp.arange(A_BLOCK)[:, None] + A_BLOCK
    ki = jnp.arange(2 * A_BLOCK)[None, :]
    rel = qi - ki
    band = (rel >= 0) & (rel < A_WINDOW)
    valid = band[None] & ((jnp.arange(nb)[:, None, None] > 0) | (ki >= A_BLOCK)[None])
    s = jnp.where(valid[None, :, None, None], s, -jnp.inf)
    sink = sinks.astype(F32).reshape(A_KVHEADS, A_REP)[None, None, :, :, None]
    m = jnp.maximum(jnp.max(s, -1), sink)
    p = jnp.exp(s - m[..., None])
    den = jnp.sum(p, -1) + jnp.exp(sink - m)
    o = jnp.einsum('bngrqk,bnkgd->bnqgrd', p, vv) / jnp.transpose(den, (0, 1, 4, 2, 3))[..., None]
    return o.reshape(B_, S_, A_QHEADS * A_HEADDIM)


def peer_ffn(x, wq, subkeys, u_tab, v_tab):
    B_, S_, D = x.shape
    T = B_ * S_
    xt = x.reshape(T, D)
    q = (xt @ wq).astype(F32).reshape(T, P_HEADS, 2, P_DKEY // 2)
    s = jnp.einsum('thcd,hckd->thck', q, subkeys.astype(F32))
    top_v, top_i = lax.top_k(s, P_TOPK)
    cand_v = top_v[:, :, 0, :, None] + top_v[:, :, 1, None, :]
    cand_i = top_i[:, :, 0, :, None] * P_NKEYS + top_i[:, :, 1, None, :]
    best_v, best_j = lax.top_k(cand_v.reshape(T, P_HEADS, P_TOPK * P_TOPK), P_TOPK)
    ids = jnp.take_along_axis(cand_i.reshape(T, P_HEADS, P_TOPK * P_TOPK), best_j, axis=-1)
    gates = jax.nn.softmax(best_v, axis=-1)
    nchunk = T // P_CHUNK

    def expert_block(args):
        xc, idc, gc = args
        u = jnp.take(u_tab, idc, axis=0)
        act = jax.nn.gelu(jnp.einsum('td,ted->te', xc, u).astype(F32), approximate=False)
        vsel = jnp.take(v_tab, idc, axis=0)
        return jnp.einsum('te,ted->td', (gc * act).astype(vsel.dtype), vsel)

    out = lax.map(expert_block, (xt.reshape(nchunk, P_CHUNK, D),
                                 ids.reshape(nchunk, P_CHUNK, P_HEADS * P_TOPK),
                                 gates.reshape(nchunk, P_CHUNK, P_HEADS * P_TOPK)))
    return out.reshape(B_, S_, D).astype(x.dtype)


def setup_inputs(seed: int = 0) -> dict:
    key = jax.random.key(seed)
    ks = jax.random.split(key, 26)
    L, D = DEPTH, D_MODEL
    beta = (8.0 * DEPTH) ** -0.25
    nrm = lambda k, shape, scale: jax.random.normal(k, shape, F32) * scale
    dt0 = jnp.exp(jax.random.uniform(ks[7], (L, S_HEADS), F32, math.log(1e-3), math.log(1e-1)))
    return {
        "x": nrm(ks[0], (BATCH, SEQ, D), 1.0),
        "w_in": nrm(ks[1], (L, D, IN_COLS), D ** -0.5),
        "mlstm_gate_b": jnp.stack([nrm(ks[2], (L, M_HEADS), 0.1),
                                   3.0 + 3.0 * jax.random.uniform(ks[3], (L, M_HEADS), F32)], axis=1),
        "mlstm_norm_w": 1.0 + nrm(ks[4], (L, M_HEADS * M_DV), 0.02),
        "ssm_conv_w": nrm(ks[5], (L, S_CONV, 1, S_CONV_CH), S_CONV ** -0.5),
        "ssm_conv_b": nrm(ks[6], (L, S_CONV_CH), 0.02),
        "ssm_dt_bias": dt0 + jnp.log(-jnp.expm1(-dt0)),
        "ssm_a_log": jnp.log(jax.random.uniform(ks[8], (L, S_HEADS), F32, 1.0, 16.0)),
        "ssm_d": 1.0 + nrm(ks[9], (L, S_HEADS), 0.02),
        "ssm_norm_w": 1.0 + nrm(ks[10], (L, S_DINNER), 0.02),
        "swa_sinks": nrm(ks[11], (L, A_QHEADS), 1.0),
        "merge_gate_b": nrm(ks[12], (L, N_BRANCH, D), 0.02),
        "w_branch": nrm(ks[13], (L, N_BRANCH, BRANCH_WIDTH, D), beta * BRANCH_WIDTH ** -0.5),
        "w_out": nrm(ks[14], (L, D, D), beta * D ** -0.5),
        "ln1_g": 1.0 + nrm(ks[15], (L, D), 0.02),
        "ln1_b": nrm(ks[16], (L, D), 0.02),
        "peer_wq": nrm(ks[17], (L, D, P_HEADS * P_DKEY), D ** -0.5),
        "peer_subkeys": nrm(ks[18], (L, P_HEADS, 2, P_NKEYS, P_DKEY // 2), (P_DKEY // 2) ** -0.5),
        "peer_u": nrm(ks[19], (L, P_EXPERTS, D), D ** -0.5),
        "peer_v": nrm(ks[20], (L, P_EXPERTS, D), beta * P_HEADS ** -0.5),
        "ln2_g": 1.0 + nrm(ks[21], (L, D), 0.02),
        "ln2_b": nrm(ks[22], (L, D), 0.02),
    }


def reference(x, w_in, mlstm_gate_b, mlstm_norm_w, ssm_conv_w, ssm_conv_b, ssm_dt_bias, ssm_a_log,
              ssm_d, ssm_norm_w, swa_sinks, merge_gate_b, w_branch, w_out, ln1_g, ln1_b,
              peer_wq, peer_subkeys, peer_u, peer_v, ln2_g, ln2_b):
    alpha = (2.0 * DEPTH) ** 0.25
    B_, S_, D = x.shape
    split_points = _in_split_points()
    for l in range(DEPTH):
        proj = x @ w_in[l]
        (mq, mk, mv, mo, mi, mf, sz, sxbc, sdt, aq, ak, av, gpre) = jnp.split(proj, split_points, axis=-1)
        y_m = mlstm_mixer(mq.reshape(B_, S_, M_HEADS, M_DQK), mk.reshape(B_, S_, M_HEADS, M_DQK),
                          mv.reshape(B_, S_, M_HEADS, M_DV), mo,
                          mi + mlstm_gate_b[l, 0], mf + mlstm_gate_b[l, 1], mlstm_norm_w[l])
        y_s = ssd_mixer(sz, sxbc, sdt, ssm_conv_w[l], ssm_conv_b[l], ssm_dt_bias[l], ssm_a_log[l],
                        ssm_d[l], ssm_norm_w[l])
        y_a = swa_mixer(aq.reshape(B_, S_, A_QHEADS, A_HEADDIM), ak.reshape(B_, S_, A_KVHEADS, A_HEADDIM),
                        av.reshape(B_, S_, A_KVHEADS, A_HEADDIM), swa_sinks[l]).astype(x.dtype)
        gates = jax.nn.sigmoid(gpre.reshape(B_, S_, N_BRANCH, D) + merge_gate_b[l])
        mix = (gates[:, :, 0] * (y_m @ w_branch[l, 0])
               + gates[:, :, 1] * (y_s @ w_branch[l, 1])
               + gates[:, :, 2] * (y_a @ w_branch[l, 2]))
        x = layer_norm(alpha * x + mix @ w_out[l], ln1_g[l], ln1_b[l])
        x = layer_norm(alpha * x + peer_ffn(x, peer_wq[l], peer_subkeys[l], peer_u[l], peer_v[l]),
                       ln2_g[l], ln2_b[l])
    return x
```

```python
import functools
import math

import jax
import jax.numpy as jnp
import numpy as np
from jax import lax
from jax.experimental import pallas as pl
from jax.experimental.pallas import tpu as pltpu

F32 = jnp.float32
BF16 = jnp.bfloat16
HIGHEST = lax.Precision.HIGHEST

LN_EPS = 1e-5
NORM_EPS = 1e-6

LANES = 128
CHUNK = 128
VMEM_LIMIT = 56 * 1024 * 1024

M_HEADS, M_DQK, M_DV = 4, 128, 256
S_HEADS, S_GROUPS, S_HPG, S_HEADDIM, S_DSTATE, S_CONV = 16, 2, 8, 64, 128, 4
S_DINNER = 1024
S_CONV_CH = S_DINNER + 2 * S_GROUPS * S_DSTATE
A_QHEADS, A_KVHEADS, A_HEADDIM, A_REP = 16, 4, 64, 4
ROPE_THETA = 10000.0
P_HEADS, P_NKEYS, P_TOPK = 8, 128, 16
P_EXPERTS = P_NKEYS * P_NKEYS
BRANCH_WIDTH = 1024

SM_I, SM_F, SM_DT = 0, M_HEADS, 2 * M_HEADS

NEG_INF = float("-inf")


def _params(*sem):
    return pltpu.CompilerParams(dimension_semantics=sem, vmem_limit_bytes=VMEM_LIMIT)


def _nt_dot(a, b, **kw):
    return lax.dot_general(a, b, (((1,), (1,)), ((), ())), preferred_element_type=F32, **kw)


def _softplus(x):
    return jnp.maximum(x, 0.0) + jnp.log(1.0 + jnp.exp(-jnp.abs(x)))


def _log_sigmoid(x):
    return jnp.minimum(x, 0.0) - jnp.log(1.0 + jnp.exp(-jnp.abs(x)))


def _sigmoid(x):
    return 1.0 / (1.0 + jnp.exp(-x))


def _silu(x):
    return x * _sigmoid(x)


def _layer_norm(y, g, b):
    mu = jnp.mean(y, -1, keepdims=True)
    d = y - mu
    var = jnp.mean(d * d, -1, keepdims=True)
    return d * lax.rsqrt(var + LN_EPS) * g + b


def _mm_kernel(x_ref, w_ref, o_ref):
    o_ref[...] = jnp.dot(x_ref[...], w_ref[...], preferred_element_type=F32).astype(o_ref.dtype)


def _matmul(x, w, tm, tn, out_dtype=F32):
    T, K = x.shape
    N = w.shape[1]
    tm, tn = min(tm, T), min(tn, N)
    return pl.pallas_call(
        _mm_kernel,
        grid=(T // tm, N // tn),
        in_specs=[pl.BlockSpec((tm, K), lambda i, j: (i, 0)),
                  pl.BlockSpec((K, tn), lambda i, j: (0, j))],
        out_specs=pl.BlockSpec((tm, tn), lambda i, j: (i, j)),
        out_shape=jax.ShapeDtypeStruct((T, N), out_dtype),
        compiler_params=_params("parallel", "arbitrary"),
        name="proj_matmul",
    )(x, w)


def _mlstm_kernel(mp_ref, sm_ref, brow_ref, nw_ref, o_ref, c_ref, n_ref, m_ref):
    L = CHUNK

    @pl.when(pl.program_id(1) == 0)
    def _():
        c_ref[...] = jnp.zeros_like(c_ref)
        n_ref[...] = jnp.zeros_like(n_ref)
        m_ref[...] = jnp.zeros_like(m_ref)

    gates = sm_ref[...] + brow_ref[...]
    row = lax.broadcasted_iota(jnp.int32, (L, L), 0)
    col = lax.broadcasted_iota(jnp.int32, (L, L), 1)
    tril = row >= col
    trilf = tril.astype(F32)
    scale = M_DQK ** -0.5
    for h in range(M_HEADS):
        q = mp_ref[:, h * M_DQK:(h + 1) * M_DQK]
        k = mp_ref[:, 512 + h * M_DQK:512 + (h + 1) * M_DQK] * scale
        v = mp_ref[:, 1024 + h * M_DV:1024 + (h + 1) * M_DV].astype(BF16)
        og = mp_ref[:, 2048 + h * M_DV:2048 + (h + 1) * M_DV]
        li_col = gates[:, SM_I + h:SM_I + h + 1]
        lf_col = _log_sigmoid(gates[:, SM_F + h:SM_F + h + 1])
        bc = jnp.dot(trilf, jnp.broadcast_to(lf_col, (L, L)), precision=HIGHEST,
                     preferred_element_type=F32)
        br = bc.T
        lir = jnp.broadcast_to(li_col, (L, L)).T
        b_col = bc[:, 0:1]
        m_prev = m_ref[h, 0:1, 0:1]
        log_d = jnp.where(tril, bc - br + lir, NEG_INF)
        m_t = jnp.maximum(jnp.max(log_d, axis=1, keepdims=True), b_col + m_prev)
        w_intra = jnp.exp(log_d - m_t)
        w_inter = jnp.exp(b_col + m_prev - m_t)
        qb = q.astype(BF16)
        s = _nt_dot(qb, k.astype(BF16)) * w_intra
        c_old = c_ref[h]
        n_old = n_ref[h, 0:1, :]
        num = (jnp.dot(s.astype(BF16), v, preferred_element_type=F32)
               + w_inter * jnp.dot(qb, c_old.astype(BF16), preferred_element_type=F32))
        den = (jnp.sum(s, axis=1, keepdims=True)
               + w_inter * jnp.sum(q * n_old, axis=1, keepdims=True))
        hh = num / jnp.maximum(jnp.abs(den), jnp.exp(-m_t))

        g = bc[L - 1:L, 0:1]
        lws = g - b_col + li_col
        m_new = jnp.maximum(g + m_prev, jnp.max(lws, axis=0, keepdims=True))
        a_prev = jnp.exp(g + m_prev - m_new)
        kw = k * jnp.exp(lws - m_new)
        c_ref[h] = a_prev * c_old + jnp.dot(kw.T.astype(BF16), v, preferred_element_type=F32)
        n_ref[h, 0:1, :] = a_prev * n_old + jnp.sum(kw, axis=0, keepdims=True)
        m_ref[h] = jnp.broadcast_to(m_new, m_ref.shape[1:])

        mu = jnp.mean(hh, -1, keepdims=True)
        d = hh - mu
        var = jnp.mean(d * d, -1, keepdims=True)
        hn = d * lax.rsqrt(var + NORM_EPS) * nw_ref[:, h * M_DV:(h + 1) * M_DV]
        o_ref[:, h * M_DV:(h + 1) * M_DV] = (_sigmoid(og) * hn).astype(o_ref.dtype)


def _mlstm(mp, sm, brow, nw, B, S):
    nc = S // CHUNK
    W = mp.shape[1]
    return pl.pallas_call(
        _mlstm_kernel,
        grid=(B, nc),
        in_specs=[pl.BlockSpec((CHUNK, W), lambda b, c: (b * nc + c, 0)),
                  pl.BlockSpec((CHUNK, LANES), lambda b, c: (b * nc + c, 0)),
                  pl.BlockSpec((1, LANES), lambda b, c: (0, 0)),
                  pl.BlockSpec((1, BRANCH_WIDTH), lambda b, c: (0, 0))],
        out_specs=pl.BlockSpec((CHUNK, BRANCH_WIDTH), lambda b, c: (b * nc + c, 0)),
        out_shape=jax.ShapeDtypeStruct((B * S, BRANCH_WIDTH), BF16),
        scratch_shapes=[pltpu.VMEM((M_HEADS, M_DQK, M_DV), F32),
                        pltpu.VMEM((M_HEADS, 8, M_DQK), F32),
                        pltpu.VMEM((M_HEADS, 8, LANES), F32)],
        compiler_params=_params("parallel", "arbitrary"),
        name="mlstm",
    )(mp, sm, brow, nw)


def _ssd_kernel(cur_ref, prev_ref, sm_ref, brow_ref, arow_ref, drow_ref, cw_ref, cb_ref, nw_ref,
                e_ref, o_ref, st_ref):
    L = CHUNK
    c = pl.program_id(1)

    @pl.when(c == 0)
    def _():
        st_ref[...] = jnp.zeros_like(st_ref)

    u = cur_ref[:, S_DINNER:]
    up = prev_ref[:, S_DINNER:] * (c > 0).astype(F32)
    rows = lax.broadcasted_iota(jnp.int32, (L, S_CONV_CH), 0)
    conv = cb_ref[...] + cw_ref[S_CONV - 1:S_CONV, :] * u
    for kk in range(S_CONV - 1):
        sh = S_CONV - 1 - kk
        shifted = jnp.where(rows < sh, pltpu.roll(up, sh, 0), pltpu.roll(u, sh, 0))
        conv = conv + cw_ref[kk:kk + 1, :] * shifted
    xbc = _silu(conv)

    dt = _softplus(sm_ref[...] + brow_ref[...])
    d_a = dt * (-jnp.exp(arow_ref[...]))
    row = lax.broadcasted_iota(jnp.int32, (L, L), 0)
    col = lax.broadcasted_iota(jnp.int32, (L, L), 1)
    tril = row >= col
    acum = jnp.dot(tril.astype(F32), d_a, precision=HIGHEST, preferred_element_type=F32)

    gw = S_HPG * S_HEADDIM
    for g in range(S_GROUPS):
        xg = xbc[:, g * gw:(g + 1) * gw]
        bg = xbc[:, S_DINNER + g * S_DSTATE:S_DINNER + (g + 1) * S_DSTATE]
        cg = xbc[:, S_DINNER + (S_GROUPS + g) * S_DSTATE:S_DINNER + (S_GROUPS + g + 1) * S_DSTATE]
        cgb = cg.astype(BF16)
        cb = _nt_dot(cgb, bg.astype(BF16))
        e_g = e_ref[g]
        dt_e = jnp.dot(dt, e_g, precision=HIGHEST, preferred_element_type=F32)
        acum_e = jnp.dot(acum, e_g, precision=HIGHEST, preferred_element_type=F32)
        xc = xg * dt_e
        xcb = xc.astype(BF16)
        last = acum_e[L - 1:L, :]
        ys = []
        for hh in range(S_HPG):
            lane = SM_DT + g * S_HPG + hh
            ac = jnp.broadcast_to(acum[:, lane:lane + 1], (L, L))
            dec = jnp.exp(jnp.where(tril, ac - ac.T, NEG_INF))
            ys.append(jnp.dot((cb * dec).astype(BF16), xcb[:, hh * S_HEADDIM:(hh + 1) * S_HEADDIM],
                              preferred_element_type=F32))
        y = jnp.concatenate(ys, axis=1)
        st = st_ref[g]
        y = y + jnp.dot(cgb, st.astype(BF16), preferred_element_type=F32) * jnp.exp(acum_e)
        upd = jnp.dot(bg.T.astype(BF16), (jnp.exp(last - acum_e) * xc).astype(BF16),
                      preferred_element_type=F32)
        st_ref[g] = jnp.exp(last) * st + upd
        y = y + drow_ref[:, g * gw:(g + 1) * gw] * xg
        y = y * _silu(cur_ref[:, g * gw:(g + 1) * gw])
        y = y * lax.rsqrt(jnp.mean(y * y, -1, keepdims=True) + NORM_EPS)
        o_ref[:, g * gw:(g + 1) * gw] = (y * nw_ref[:, g * gw:(g + 1) * gw]).astype(o_ref.dtype)


def _ssd(sp, sm, brow, arow, drow, cw, cb, nw, e_mat, B, S):
    nc = S // CHUNK
    W = sp.shape[1]
    const = lambda b, c: (0, 0)
    return pl.pallas_call(
        _ssd_kernel,
        grid=(B, nc),
        in_specs=[pl.BlockSpec((CHUNK, W), lambda b, c: (b * nc + c, 0)),
                  pl.BlockSpec((CHUNK, W), lambda b, c: (b * nc + jnp.maximum(c - 1, 0), 0)),
                  pl.BlockSpec((CHUNK, LANES), lambda b, c: (b * nc + c, 0)),
                  pl.BlockSpec((1, LANES), const),
                  pl.BlockSpec((1, LANES), const),
                  pl.BlockSpec((1, S_DINNER), const),
                  pl.BlockSpec((S_CONV, S_CONV_CH), const),
                  pl.BlockSpec((1, S_CONV_CH), const),
                  pl.BlockSpec((1, S_DINNER), const),
                  pl.BlockSpec((S_GROUPS, LANES, S_HPG * S_HEADDIM), lambda b, c: (0, 0, 0))],
        out_specs=pl.BlockSpec((CHUNK, S_DINNER), lambda b, c: (b * nc + c, 0)),
        out_shape=jax.ShapeDtypeStruct((B * S, S_DINNER), BF16),
        scratch_shapes=[pltpu.VMEM((S_GROUPS, S_DSTATE, S_HPG * S_HEADDIM), F32)],
        compiler_params=_params("parallel", "arbitrary"),
        name="ssd",
    )(sp, sp, sm, brow, arow, drow, cw, cb, nw, e_mat)


def _rope(x, cos, sin):
    lane = lax.broadcasted_iota(jnp.int32, x.shape, 1)
    first_half = (lane % A_HEADDIM) < (A_HEADDIM // 2)
    rot = jnp.where(first_half, -pltpu.roll(x, LANES - A_HEADDIM // 2, 1),
                    pltpu.roll(x, A_HEADDIM // 2, 1))
    return x * cos + rot * sin


def _swa_kernel(cur_ref, prev_ref, cos_ref, sin_ref, cosp_ref, sinp_ref, sink_ref, o_ref):
    L = CHUNK
    n = pl.program_id(1)
    cos, sin = cos_ref[...], sin_ref[...]
    cosp, sinp = cosp_ref[...], sinp_ref[...]
    qo, ko, vo = 0, A_QHEADS * A_HEADDIM, (A_QHEADS + A_KVHEADS) * A_HEADDIM
    kc = jnp.concatenate([_rope(cur_ref[:, ko + j * LANES:ko + (j + 1) * LANES], cos, sin)
                          for j in range(2)], axis=1).astype(BF16)
    kp = jnp.concatenate([_rope(prev_ref[:, ko + j * LANES:ko + (j + 1) * LANES], cosp, sinp)
                          for j in range(2)], axis=1).astype(BF16)
    vc = cur_ref[:, vo:vo + A_KVHEADS * A_HEADDIM].astype(BF16)
    vp = prev_ref[:, vo:vo + A_KVHEADS * A_HEADDIM].astype(BF16)
    row = lax.broadcasted_iota(jnp.int32, (L, L), 0)
    col = lax.broadcasted_iota(jnp.int32, (L, L), 1)
    cur_ok = col <= row
    prev_ok = jnp.logical_and(col > row, n > 0)
    scale = A_HEADDIM ** -0.5
    for j in range(A_QHEADS // 2):
        qpair = _rope(cur_ref[:, qo + j * LANES:qo + (j + 1) * LANES], cos, sin).astype(BF16)
        outs = []
        for t in range(2):
            hq = 2 * j + t
            g = hq // A_REP
            qh = qpair[:, t * A_HEADDIM:(t + 1) * A_HEADDIM]
            sl = slice(g * A_HEADDIM, (g + 1) * A_HEADDIM)
            s_c = jnp.where(cur_ok, _nt_dot(qh, kc[:, sl]) * scale, NEG_INF)
            s_p = jnp.where(prev_ok, _nt_dot(qh, kp[:, sl]) * scale, NEG_INF)
            sink = sink_ref[:, hq:hq + 1]
            m = jnp.maximum(jnp.maximum(jnp.max(s_c, axis=1, keepdims=True),
                                        jnp.max(s_p, axis=1, keepdims=True)), sink)
            p_c = jnp.exp(s_c - m)
            p_p = jnp.exp(s_p - m)
            den = (jnp.sum(p_c, axis=1, keepdims=True) + jnp.sum(p_p, axis=1, keepdims=True)
                   + jnp.exp(sink - m))
            o = (jnp.dot(p_c.astype(BF16), vc[:, sl], preferred_element_type=F32)
                 + jnp.dot(p_p.astype(BF16), vp[:, sl], preferred_element_type=F32))
            outs.append(o / den)
        o_ref[:, j * LANES:(j + 1) * LANES] = jnp.concatenate(outs, axis=1).astype(o_ref.dtype)


def _swa(ap, cos_t, sin_t, sink_row, B, S):
    nb = S // CHUNK
    W = ap.shape[1]
    cur = lambda b, n: (b * nb + n, 0)
    prev = lambda b, n: (b * nb + jnp.maximum(n - 1, 0), 0)
    tab = lambda b, n: (n, 0)
    tabp = lambda b, n: (jnp.maximum(n - 1, 0), 0)
    return pl.pallas_call(
        _swa_kernel,
        grid=(B, nb),
        in_specs=[pl.BlockSpec((CHUNK, W), cur), pl.BlockSpec((CHUNK, W), prev),
                  pl.BlockSpec((CHUNK, LANES), tab), pl.BlockSpec((CHUNK, LANES), tab),
                  pl.BlockSpec((CHUNK, LANES), tabp), pl.BlockSpec((CHUNK, LANES), tabp),
                  pl.BlockSpec((1, LANES), lambda b, n: (0, 0))],
        out_specs=pl.BlockSpec((CHUNK, BRANCH_WIDTH), cur),
        out_shape=jax.ShapeDtypeStruct((B * S, BRANCH_WIDTH), BF16),
        compiler_params=_params("parallel", "parallel"),
        name="swa",
    )(ap, ap, cos_t, sin_t, cos_t, sin_t, sink_row)


def _merge_kernel(ym_ref, ys_ref, ya_ref, g0_ref, g1_ref, g2_ref, b0_ref, b1_ref, b2_ref,
                  w0_ref, w1_ref, w2_ref, o_ref):
    acc = None
    for y_ref, g_ref, b_ref, w_ref in ((ym_ref, g0_ref, b0_ref, w0_ref),
                                       (ys_ref, g1_ref, b1_ref, w1_ref),
                                       (ya_ref, g2_ref, b2_ref, w2_ref)):
        t = _sigmoid(g_ref[...] + b_ref[...]) * jnp.dot(y_ref[...], w_ref[0],
                                                        preferred_element_type=F32)
        acc = t if acc is None else acc + t
    o_ref[...] = acc.astype(o_ref.dtype)


def _merge(ym, ys, ya, gp, gb, wb, tm, tn):
    T = ym.shape[0]
    D = wb.shape[2]
    tm, tn = min(tm, T), min(tn, D)
    nj = D // tn
    yspec = pl.BlockSpec((tm, BRANCH_WIDTH), lambda i, j: (i, 0))
    gspec = lambda k: pl.BlockSpec((tm, tn), lambda i, j: (i, k * nj + j))
    bspec = lambda k: pl.BlockSpec((1, tn), lambda i, j: (0, k * nj + j))
    wspec = lambda k: pl.BlockSpec((1, BRANCH_WIDTH, tn), lambda i, j: (k, 0, j))
    return pl.pallas_call(
        _merge_kernel,
        grid=(T // tm, nj),
        in_specs=[yspec, yspec, yspec, gspec(0), gspec(1), gspec(2), bspec(0), bspec(1), bspec(2),
                  wspec(0), wspec(1), wspec(2)],
        out_specs=pl.BlockSpec((tm, tn), lambda i, j: (i, j)),
        out_shape=jax.ShapeDtypeStruct((T, D), BF16),
        compiler_params=_params("parallel", "arbitrary"),
        name="merge",
    )(ym, ys, ya, gp, gp, gp, gb, gb, gb, wb, wb, wb)


def _outproj_kernel(x_ref, mix_ref, w_ref, g_ref, b_ref, o_ref, ob_ref, *, alpha):
    y = alpha * x_ref[...] + jnp.dot(mix_ref[...], w_ref[...], preferred_element_type=F32)
    r = _layer_norm(y, g_ref[...], b_ref[...])
    o_ref[...] = r
    ob_ref[...] = r.astype(BF16)


def _outproj(x, mix, w, g, b, alpha, tm):
    T, D = x.shape
    tm = min(tm, T)
    rowspec = pl.BlockSpec((tm, D), lambda i: (i, 0))
    vec = pl.BlockSpec((1, D), lambda i: (0, 0))
    return pl.pallas_call(
        functools.partial(_outproj_kernel, alpha=alpha),
        grid=(T // tm,),
        in_specs=[rowspec, rowspec, pl.BlockSpec((D, D), lambda i: (0, 0)), vec, vec],
        out_specs=[rowspec, rowspec],
        out_shape=[jax.ShapeDtypeStruct((T, D), F32), jax.ShapeDtypeStruct((T, D), BF16)],
        compiler_params=_params("parallel"),
        name="outproj_ln",
    )(x, mix, w, g, b)


def _res_ln_kernel(x_ref, y_ref, g_ref, b_ref, o_ref, ob_ref, *, alpha):
    r = _layer_norm(alpha * x_ref[...] + y_ref[...], g_ref[...], b_ref[...])
    o_ref[...] = r
    ob_ref[...] = r.astype(BF16)


def _res_ln(x, y, g, b, alpha, tm):
    T, D = x.shape
    tm = min(tm, T)
    rowspec = pl.BlockSpec((tm, D), lambda i: (i, 0))
    vec = pl.BlockSpec((1, D), lambda i: (0, 0))
    return pl.pallas_call(
        functools.partial(_res_ln_kernel, alpha=alpha),
        grid=(T // tm,),
        in_specs=[rowspec, rowspec, vec, vec],
        out_specs=[rowspec, rowspec],
        out_shape=[jax.ShapeDtypeStruct((T, D), F32), jax.ShapeDtypeStruct((T, D), BF16)],
        compiler_params=_params("parallel"),
        name="res_ln",
    )(x, y, g, b)


_CAND_NQ = (16, 8, 5, 4, 3, 2, 2, 2)
_CAND_ROWS = 16 + 7 * 8 + 8


def _extract_top(work, iota, n_iter, on_pick):
    big = float(work.shape[0])
    for p in range(n_iter):
        m = jnp.max(work, axis=0, keepdims=True)
        idx = jnp.min(jnp.where(work == m, iota, big), axis=0, keepdims=True)
        onehot = iota == idx
        work = jnp.where(onehot, NEG_INF, work)
        on_pick(p, m, onehot)


def _peer_select_kernel(xb_ref, wq_ref, sk_ref, rank2_ref, cnt1_ref, e1_ref, e2_ref,
                        q_scr, top_scr, cnt_scr):
    h = pl.program_id(1)
    tm = xb_ref.shape[0]
    K = P_TOPK

    @pl.when(h == 0)
    def _():
        q = jnp.dot(xb_ref[...], wq_ref[...], preferred_element_type=F32)
        for hc in range(2 * P_HEADS):
            q_scr[hc] = q[:, hc * LANES:(hc + 1) * LANES]

    iota = lax.broadcasted_iota(jnp.int32, (P_NKEYS, tm), 0).astype(F32)
    scores, ranks = [], []
    for c in range(2):
        s_t = _nt_dot(sk_ref[0, c], q_scr[2 * h + c], precision=HIGHEST)
        rank_holder = [jnp.full((P_NKEYS, tm), float(K), F32)]

        def on_pick(p, m, onehot, c=c, rank_holder=rank_holder):
            top_scr[c, p:p + 1, :] = m
            rank_holder[0] = jnp.where(onehot, float(p), rank_holder[0])

        _extract_top(s_t, iota, K, on_pick)
        scores.append(s_t)
        ranks.append(rank_holder[0])

    a = top_scr[0]
    b = top_scr[1]
    ea = jnp.exp(a - a[0:1])
    eb = jnp.exp(b - b[0:1])
    q8 = lax.broadcasted_iota(jnp.int32, (8, tm), 0)
    cand = [a[0:1] + b]
    wcand = [ea[0:1] * eb]
    for p in range(1, 8):
        ok = q8 < _CAND_NQ[p]
        cand.append(jnp.where(ok, a[p:p + 1] + b[0:8], NEG_INF))
        wcand.append(ea[p:p + 1] * eb[0:8])
    cand.append(a[8:16] + b[0:1])
    wcand.append(ea[8:16] * eb[0:1])
    cand = jnp.concatenate(cand, axis=0)
    wcand = jnp.concatenate(wcand, axis=0)
    iota_c = lax.broadcasted_iota(jnp.int32, (_CAND_ROWS, tm), 0).astype(F32)
    sel_holder = [jnp.zeros((_CAND_ROWS, tm), F32)]

    def on_pick2(p, m, onehot):
        sel_holder[0] = jnp.where(onehot, 1.0, sel_holder[0])

    _extract_top(cand, iota_c, K, on_pick2)
    sel = sel_holder[0]
    z = jnp.sum(sel * wcand, axis=0, keepdims=True)
    cnt_scr[0:1, :] = jnp.sum(sel[0:16], axis=0, keepdims=True)
    for p in range(1, 8):
        cnt_scr[p:p + 1, :] = jnp.sum(sel[8 + 8 * p:16 + 8 * p], axis=0, keepdims=True)
    cnt_scr[8:16, :] = sel[72:80]
    cnt = cnt_scr[...]
    cnt1 = jnp.zeros((P_NKEYS, tm), F32)
    for p in range(K):
        cnt1 = jnp.where(ranks[0] == float(p), cnt[p:p + 1], cnt1)
    rank2_ref[0] = ranks[1]
    cnt1_ref[0] = cnt1
    e1_ref[0] = jnp.exp(scores[0] - a[0:1])
    e2_ref[0] = jnp.exp(scores[1] - b[0:1]) / z


def _peer_select(xb, wq, sk, tm):
    T, D = xb.shape
    tm = min(tm, T)
    out = jax.ShapeDtypeStruct((P_HEADS, P_NKEYS, T), F32)
    ospec = pl.BlockSpec((1, P_NKEYS, tm), lambda i, h: (h, 0, i))
    return pl.pallas_call(
        _peer_select_kernel,
        grid=(T // tm, P_HEADS),
        in_specs=[pl.BlockSpec((tm, D), lambda i, h: (i, 0)),
                  pl.BlockSpec(wq.shape, lambda i, h: (0, 0)),
                  pl.BlockSpec((1, 2, P_NKEYS, LANES), lambda i, h: (h, 0, 0, 0))],
        out_specs=[ospec, ospec, ospec, ospec],
        out_shape=[out, out, out, out],
        scratch_shapes=[pltpu.VMEM((2 * P_HEADS, tm, LANES), F32),
                        pltpu.VMEM((2, P_TOPK, tm), F32),
                        pltpu.VMEM((P_TOPK, tm), F32)],
        compiler_params=_params("parallel", "arbitrary"),
        name="peer_select",
    )(xb, wq, sk)


def _peer_dense_kernel(xb_ref, rank2_ref, cnt1_ref, e1_ref, e2_ref, u_ref, vt_ref, o_ref, acc_ref,
                       *, nsub):
    j = pl.program_id(1)

    @pl.when(j == 0)
    def _():
        acc_ref[...] = jnp.zeros_like(acc_ref)

    act_t = _nt_dot(u_ref[...], xb_ref[...])
    parts = []
    for ii in range(nsub):
        i = j * nsub + ii
        w = None
        for h in range(P_HEADS):
            cnt = cnt1_ref[h, pl.ds(i, 1), :]
            e1 = e1_ref[h, pl.ds(i, 1), :]
            t = jnp.where(rank2_ref[h] < cnt, e2_ref[h], 0.0) * e1
            w = t if w is None else w + t
        a = act_t[ii * P_NKEYS:(ii + 1) * P_NKEYS]
        gelu = 0.5 * a * (1.0 + lax.erf(a * (0.5 ** 0.5)))
        parts.append((w * gelu).astype(BF16))
    p_t = jnp.concatenate(parts, axis=0) if nsub > 1 else parts[0]
    acc_ref[...] += jnp.dot(vt_ref[...], p_t, preferred_element_type=F32)

    @pl.when(j == pl.num_programs(1) - 1)
    def _():
        o_ref[...] = acc_ref[...].T


def _peer_dense(xb, rank2, cnt1, e1, e2, u_b, vt_b, tm, be):
    T, D = xb.shape
    tm = min(tm, T)
    nsub = be // P_NKEYS
    sel = pl.BlockSpec((P_HEADS, P_NKEYS, tm), lambda i, j: (0, 0, i))
    return pl.pallas_call(
        functools.partial(_peer_dense_kernel, nsub=nsub),
        grid=(T // tm, P_EXPERTS // be),
        in_specs=[pl.BlockSpec((tm, D), lambda i, j: (i, 0)), sel, sel, sel, sel,
                  pl.BlockSpec((be, D), lambda i, j: (j, 0)),
                  pl.BlockSpec((D, be), lambda i, j: (0, j))],
        out_specs=pl.BlockSpec((tm, D), lambda i, j: (i, 0)),
        out_shape=jax.ShapeDtypeStruct((T, D), F32),
        scratch_shapes=[pltpu.VMEM((D, tm), F32)],
        compiler_params=_params("parallel", "arbitrary"),
        name="peer_dense",
    )(xb, rank2, cnt1, e1, e2, u_b, vt_b)


def _pad_row(vals_at, width=LANES):
    row = jnp.zeros((1, width), F32)
    for off, v in vals_at:
        row = row.at[0, off:off + v.shape[0]].set(v.astype(F32))
    return row


def _rope_tables(S):
    half = A_HEADDIM // 2
    freqs = ROPE_THETA ** (-jnp.arange(half, dtype=F32) / half)
    ang = jnp.arange(S, dtype=F32)[:, None] * freqs[None, :]
    reps = LANES // half
    return jnp.tile(jnp.cos(ang), (1, reps)), jnp.tile(jnp.sin(ang), (1, reps))


def _head_expand_matrix():
    e = np.zeros((S_GROUPS, LANES, S_HPG * S_HEADDIM), np.float32)
    for g in range(S_GROUPS):
        for hh in range(S_HPG):
            e[g, SM_DT + g * S_HPG + hh, hh * S_HEADDIM:(hh + 1) * S_HEADDIM] = 1.0
    return jnp.asarray(e)


def _layer(x, xb, p, consts, B, S, alpha):
    T, D = x.shape
    w_in = p["w_in"]
    o_m = 2 * M_HEADS * M_DQK + 2 * M_HEADS * M_DV
    o_s = o_m + 2 * M_HEADS
    o_dt = o_s + S_DINNER + S_CONV_CH
    o_a = o_dt + S_HEADS
    o_g = o_a + (A_QHEADS + 2 * A_KVHEADS) * A_HEADDIM
    w_m = w_in[:, :o_m].astype(BF16)
    w_s = w_in[:, o_s:o_dt].astype(BF16)
    w_a = w_in[:, o_a:o_g].astype(BF16)
    w_g = w_in[:, o_g:].astype(BF16)
    w_sm = jnp.concatenate([w_in[:, o_m:o_s], w_in[:, o_dt:o_a],
                            jnp.zeros((D, LANES - 2 * M_HEADS - S_HEADS), F32)], axis=1).astype(BF16)

    mp = _matmul(xb, w_m, 1024, 512)
    sp = _matmul(xb, w_s, 1024, 512)
    ap = _matmul(xb, w_a, 1024, 512)
    gp = _matmul(xb, w_g, 1024, 512)
    sm = _matmul(xb, w_sm, 1024, LANES)

    brow = _pad_row([(SM_I, p["mlstm_gate_b"][0]), (SM_F, p["mlstm_gate_b"][1]),
                     (SM_DT, p["ssm_dt_bias"])])
    arow = _pad_row([(SM_DT, p["ssm_a_log"])])
    drow = jnp.repeat(p["ssm_d"].astype(F32), S_HEADDIM)[None, :]
    y_m = _mlstm(mp, sm, brow, p["mlstm_norm_w"][None, :], B, S)
    y_s = _ssd(sp, sm, brow, arow, drow, p["ssm_conv_w"][:, 0, :], p["ssm_conv_b"][None, :],
               p["ssm_norm_w"][None, :], consts["e_mat"], B, S)
    y_a = _swa(ap, consts["cos"], consts["sin"], _pad_row([(0, p["swa_sinks"])]), B, S)

    mix = _merge(y_m, y_s, y_a, gp, p["merge_gate_b"].reshape(1, 3 * D),
                 p["w_branch"].astype(BF16), 1024, 512)
    x1, x1b = _outproj(x, mix, p["w_out"].astype(BF16), p["ln1_g"][None, :], p["ln1_b"][None, :],
                       alpha, 512)

    rank2, cnt1, e1, e2 = _peer_select(x1b, p["peer_wq"].astype(BF16), p["peer_subkeys"], 512)
    peer = _peer_dense(x1b, rank2, cnt1, e1, e2, p["peer_u"].astype(BF16),
                       p["peer_v"].T.astype(BF16), 512, 512)
    return _res_ln(x1, peer, p["ln2_g"][None, :], p["ln2_b"][None, :], alpha, 512)


def _forward(x, params, depth):
    B, S, D = x.shape
    alpha = (2.0 * depth) ** 0.25
    cos_t, sin_t = _rope_tables(S)
    consts = {"cos": cos_t, "sin": sin_t, "e_mat": _head_expand_matrix()}
    xf = x.reshape(B * S, D)

    def body(carry, p):
        xc, xcb = carry
        return _layer(xc, xcb, p, consts, B, S, alpha), None

    (xf, _), _ = lax.scan(body, (xf, xf.astype(BF16)), params)
    return xf.reshape(B, S, D)


def kernel(x, w_in, mlstm_gate_b, mlstm_norm_w, ssm_conv_w, ssm_conv_b, ssm_dt_bias, ssm_a_log,
           ssm_d, ssm_norm_w, swa_sinks, merge_gate_b, w_branch, w_out, ln1_g, ln1_b,
           peer_wq, peer_subkeys, peer_u, peer_v, ln2_g, ln2_b):
    params = dict(w_in=w_in, mlstm_gate_b=mlstm_gate_b, mlstm_norm_w=mlstm_norm_w,
                  ssm_conv_w=ssm_conv_w, ssm_conv_b=ssm_conv_b, ssm_dt_bias=ssm_dt_bias,
                  ssm_a_log=ssm_a_log, ssm_d=ssm_d, ssm_norm_w=ssm_norm_w, swa_sinks=swa_sinks,
                  merge_gate_b=merge_gate_b, w_branch=w_branch, w_out=w_out, ln1_g=ln1_g,
                  ln1_b=ln1_b, peer_wq=peer_wq, peer_subkeys=peer_subkeys, peer_u=peer_u,
                  peer_v=peer_v, ln2_g=ln2_g, ln2_b=ln2_b)
    return _forward(x, params, w_in.shape[0])
```

```python
import functools
import math

import jax
import jax.numpy as jnp
import numpy as np
from jax import lax
from jax.experimental import pallas as pl
from jax.experimental.pallas import tpu as pltpu

F32 = jnp.float32
BF16 = jnp.bfloat16
HIGHEST = lax.Precision.HIGHEST

LN_EPS = 1e-5
NORM_EPS = 1e-6

LANES = 128
CHUNK = 128
VMEM_LIMIT = 56 * 1024 * 1024

M_HEADS, M_DQK, M_DV = 4, 128, 256
S_HEADS, S_GROUPS, S_HPG, S_HEADDIM, S_DSTATE, S_CONV = 16, 2, 8, 64, 128, 4
S_DINNER = 1024
S_CONV_CH = S_DINNER + 2 * S_GROUPS * S_DSTATE
A_QHEADS, A_KVHEADS, A_HEADDIM, A_REP = 16, 4, 64, 4
ROPE_THETA = 10000.0
P_HEADS, P_NKEYS, P_TOPK = 8, 128, 16
P_EXPERTS = P_NKEYS * P_NKEYS
BRANCH_WIDTH = 1024

SM_I, SM_F, SM_DT = 0, M_HEADS, 2 * M_HEADS

NEG_INF = float("-inf")


def _params(*sem, flags=None):
    return pltpu.CompilerParams(dimension_semantics=sem, vmem_limit_bytes=VMEM_LIMIT, flags=flags)


def _nt_dot(a, b, **kw):
    return lax.dot_general(a, b, (((1,), (1,)), ((), ())), preferred_element_type=F32, **kw)


def _softplus(x):
    return jnp.maximum(x, 0.0) + jnp.log(1.0 + jnp.exp(-jnp.abs(x)))


def _log_sigmoid(x):
    return jnp.minimum(x, 0.0) - jnp.log(1.0 + jnp.exp(-jnp.abs(x)))


def _sigmoid(x):
    return 1.0 / (1.0 + jnp.exp(-x))


def _silu(x):
    return x * _sigmoid(x)


def _layer_norm(y, g, b):
    mu = jnp.mean(y, -1, keepdims=True)
    d = y - mu
    var = jnp.mean(d * d, -1, keepdims=True)
    return d * lax.rsqrt(var + LN_EPS) * g + b


def _mm_kernel(x_ref, w_ref, o_ref):
    o_ref[...] = jnp.dot(x_ref[...], w_ref[...], preferred_element_type=F32).astype(o_ref.dtype)


def _matmul(x, w, tm, tn, out_dtype=F32):
    T, K = x.shape
    N = w.shape[1]
    tm, tn = min(tm, T), min(tn, N)
    return pl.pallas_call(
        _mm_kernel,
        grid=(T // tm, N // tn),
        in_specs=[pl.BlockSpec((tm, K), lambda i, j: (i, 0)),
                  pl.BlockSpec((K, tn), lambda i, j: (0, j))],
        out_specs=pl.BlockSpec((tm, tn), lambda i, j: (i, j)),
        out_shape=jax.ShapeDtypeStruct((T, N), out_dtype),
        compiler_params=_params("parallel", "arbitrary"),
        name="proj_matmul",
    )(x, w)


def _mlstm_kernel(mp_ref, sm_ref, brow_ref, nw_ref, o_ref, c_ref, n_ref, m_ref):
    L = CHUNK

    @pl.when(pl.program_id(1) == 0)
    def _():
        c_ref[...] = jnp.zeros_like(c_ref)
        n_ref[...] = jnp.zeros_like(n_ref)
        m_ref[...] = jnp.zeros_like(m_ref)

    gates = sm_ref[...] + brow_ref[...]
    row = lax.broadcasted_iota(jnp.int32, (L, L), 0)
    col = lax.broadcasted_iota(jnp.int32, (L, L), 1)
    tril = row >= col
    trilf = tril.astype(F32)
    scale = M_DQK ** -0.5
    for h in range(M_HEADS):
        q = mp_ref[:, h * M_DQK:(h + 1) * M_DQK]
        k = mp_ref[:, 512 + h * M_DQK:512 + (h + 1) * M_DQK] * scale
        v = mp_ref[:, 1024 + h * M_DV:1024 + (h + 1) * M_DV].astype(BF16)
        og = mp_ref[:, 2048 + h * M_DV:2048 + (h + 1) * M_DV]
        li_col = gates[:, SM_I + h:SM_I + h + 1]
        lf_col = _log_sigmoid(gates[:, SM_F + h:SM_F + h + 1])
        bc = jnp.dot(trilf, jnp.broadcast_to(lf_col, (L, L)), precision=HIGHEST,
                     preferred_element_type=F32)
        br = bc.T
        lir = jnp.broadcast_to(li_col, (L, L)).T
        b_col = bc[:, 0:1]
        m_prev = m_ref[h, 0:1, 0:1]
        log_d = jnp.where(tril, bc - br + lir, NEG_INF)
        m_t = jnp.maximum(jnp.max(log_d, axis=1, keepdims=True), b_col + m_prev)
        w_intra = jnp.exp(log_d - m_t)
        w_inter = jnp.exp(b_col + m_prev - m_t)
        qb = q.astype(BF16)
        s = _nt_dot(qb, k.astype(BF16)) * w_intra
        c_old = c_ref[h]
        n_old = n_ref[h, 0:1, :]
        num = (jnp.dot(s.astype(BF16), v, preferred_element_type=F32)
               + w_inter * jnp.dot(qb, c_old.astype(BF16), preferred_element_type=F32))
        den = (jnp.sum(s, axis=1, keepdims=True)
               + w_inter * jnp.sum(q * n_old, axis=1, keepdims=True))
        hh = num / jnp.maximum(jnp.abs(den), jnp.exp(-m_t))

        g = bc[L - 1:L, 0:1]
        lws = g - b_col + li_col
        m_new = jnp.maximum(g + m_prev, jnp.max(lws, axis=0, keepdims=True))
        a_prev = jnp.exp(g + m_prev - m_new)
        kw = k * jnp.exp(lws - m_new)
        c_ref[h] = a_prev * c_old + jnp.dot(kw.T.astype(BF16), v, preferred_element_type=F32)
        n_ref[h, 0:1, :] = a_prev * n_old + jnp.sum(kw, axis=0, keepdims=True)
        m_ref[h] = jnp.broadcast_to(m_new, m_ref.shape[1:])

        mu = jnp.mean(hh, -1, keepdims=True)
        d = hh - mu
        var = jnp.mean(d * d, -1, keepdims=True)
        hn = d * lax.rsqrt(var + NORM_EPS) * nw_ref[:, h * M_DV:(h + 1) * M_DV]
        o_ref[:, h * M_DV:(h + 1) * M_DV] = (_sigmoid(og) * hn).astype(o_ref.dtype)


def _mlstm(mp, sm, brow, nw, B, S):
    nc = S // CHUNK
    W = mp.shape[1]
    return pl.pallas_call(
        _mlstm_kernel,
        grid=(B, nc),
        in_specs=[pl.BlockSpec((CHUNK, W), lambda b, c: (b * nc + c, 0)),
                  pl.BlockSpec((CHUNK, LANES), lambda b, c: (b * nc + c, 0)),
                  pl.BlockSpec((1, LANES), lambda b, c: (0, 0)),
                  pl.BlockSpec((1, BRANCH_WIDTH), lambda b, c: (0, 0))],
        out_specs=pl.BlockSpec((CHUNK, BRANCH_WIDTH), lambda b, c: (b * nc + c, 0)),
        out_shape=jax.ShapeDtypeStruct((B * S, BRANCH_WIDTH), BF16),
        scratch_shapes=[pltpu.VMEM((M_HEADS, M_DQK, M_DV), F32),
                        pltpu.VMEM((M_HEADS, 8, M_DQK), F32),
                        pltpu.VMEM((M_HEADS, 8, LANES), F32)],
        compiler_params=_params("parallel", "arbitrary"),
        name="mlstm",
    )(mp, sm, brow, nw)


def _ssd_kernel(cur_ref, prev_ref, sm_ref, brow_ref, arow_ref, drow_ref, cw_ref, cb_ref, nw_ref,
                e_ref, o_ref, st_ref):
    L = CHUNK
    c = pl.program_id(1)

    @pl.when(c == 0)
    def _():
        st_ref[...] = jnp.zeros_like(st_ref)

    u = cur_ref[:, S_DINNER:]
    up = prev_ref[:, S_DINNER:] * (c > 0).astype(F32)
    rows = lax.broadcasted_iota(jnp.int32, (L, S_CONV_CH), 0)
    conv = cb_ref[...] + cw_ref[S_CONV - 1:S_CONV, :] * u
    for kk in range(S_CONV - 1):
        sh = S_CONV - 1 - kk
        shifted = jnp.where(rows < sh, pltpu.roll(up, sh, 0), pltpu.roll(u, sh, 0))
        conv = conv + cw_ref[kk:kk + 1, :] * shifted
    xbc = _silu(conv)

    dt = _softplus(sm_ref[...] + brow_ref[...])
    d_a = dt * (-jnp.exp(arow_ref[...]))
    row = lax.broadcasted_iota(jnp.int32, (L, L), 0)
    col = lax.broadcasted_iota(jnp.int32, (L, L), 1)
    tril = row >= col
    acum = jnp.dot(tril.astype(F32), d_a, precision=HIGHEST, preferred_element_type=F32)

    gw = S_HPG * S_HEADDIM
    for g in range(S_GROUPS):
        xg = xbc[:, g * gw:(g + 1) * gw]
        bg = xbc[:, S_DINNER + g * S_DSTATE:S_DINNER + (g + 1) * S_DSTATE]
        cg = xbc[:, S_DINNER + (S_GROUPS + g) * S_DSTATE:S_DINNER + (S_GROUPS + g + 1) * S_DSTATE]
        cgb = cg.astype(BF16)
        cb = _nt_dot(cgb, bg.astype(BF16))
        e_g = e_ref[g]
        dt_e = jnp.dot(dt, e_g, precision=HIGHEST, preferred_element_type=F32)
        acum_e = jnp.dot(acum, e_g, precision=HIGHEST, preferred_element_type=F32)
        xc = xg * dt_e
        xcb = xc.astype(BF16)
        last = acum_e[L - 1:L, :]
        ys = []
        for hh in range(S_HPG):
            lane = SM_DT + g * S_HPG + hh
            ac = jnp.broadcast_to(acum[:, lane:lane + 1], (L, L))
            dec = jnp.exp(jnp.where(tril, ac - ac.T, NEG_INF))
            ys.append(jnp.dot((cb * dec).astype(BF16), xcb[:, hh * S_HEADDIM:(hh + 1) * S_HEADDIM],
                              preferred_element_type=F32))
        y = jnp.concatenate(ys, axis=1)
        st = st_ref[g]
        y = y + jnp.dot(cgb, st.astype(BF16), preferred_element_type=F32) * jnp.exp(acum_e)
        upd = jnp.dot(bg.T.astype(BF16), (jnp.exp(last - acum_e) * xc).astype(BF16),
                      preferred_element_type=F32)
        st_ref[g] = jnp.exp(last) * st + upd
        y = y + drow_ref[:, g * gw:(g + 1) * gw] * xg
        y = y * _silu(cur_ref[:, g * gw:(g + 1) * gw])
        y = y * lax.rsqrt(jnp.mean(y * y, -1, keepdims=True) + NORM_EPS)
        o_ref[:, g * gw:(g + 1) * gw] = (y * nw_ref[:, g * gw:(g + 1) * gw]).astype(o_ref.dtype)


def _ssd(sp, sm, brow, arow, drow, cw, cb, nw, e_mat, B, S):
    nc = S // CHUNK
    W = sp.shape[1]
    const = lambda b, c: (0, 0)
    return pl.pallas_call(
        _ssd_kernel,
        grid=(B, nc),
        in_specs=[pl.BlockSpec((CHUNK, W), lambda b, c: (b * nc + c, 0)),
                  pl.BlockSpec((CHUNK, W), lambda b, c: (b * nc + jnp.maximum(c - 1, 0), 0)),
                  pl.BlockSpec((CHUNK, LANES), lambda b, c: (b * nc + c, 0)),
                  pl.BlockSpec((1, LANES), const),
                  pl.BlockSpec((1, LANES), const),
                  pl.BlockSpec((1, S_DINNER), const),
                  pl.BlockSpec((S_CONV, S_CONV_CH), const),
                  pl.BlockSpec((1, S_CONV_CH), const),
                  pl.BlockSpec((1, S_DINNER), const),
                  pl.BlockSpec((S_GROUPS, LANES, S_HPG * S_HEADDIM), lambda b, c: (0, 0, 0))],
        out_specs=pl.BlockSpec((CHUNK, S_DINNER), lambda b, c: (b * nc + c, 0)),
        out_shape=jax.ShapeDtypeStruct((B * S, S_DINNER), BF16),
        scratch_shapes=[pltpu.VMEM((S_GROUPS, S_DSTATE, S_HPG * S_HEADDIM), F32)],
        compiler_params=_params("parallel", "arbitrary"),
        name="ssd",
    )(sp, sp, sm, brow, arow, drow, cw, cb, nw, e_mat)


def _rope(x, cos, sin):
    lane = lax.broadcasted_iota(jnp.int32, x.shape, 1)
    first_half = (lane % A_HEADDIM) < (A_HEADDIM // 2)
    rot = jnp.where(first_half, -pltpu.roll(x, LANES - A_HEADDIM // 2, 1),
                    pltpu.roll(x, A_HEADDIM // 2, 1))
    return x * cos + rot * sin


def _swa_kernel(cur_ref, prev_ref, cos_ref, sin_ref, cosp_ref, sinp_ref, sink_ref, o_ref):
    L = CHUNK
    n = pl.program_id(1)
    cos, sin = cos_ref[...], sin_ref[...]
    cosp, sinp = cosp_ref[...], sinp_ref[...]
    qo, ko, vo = 0, A_QHEADS * A_HEADDIM, (A_QHEADS + A_KVHEADS) * A_HEADDIM
    kc = jnp.concatenate([_rope(cur_ref[:, ko + j * LANES:ko + (j + 1) * LANES], cos, sin)
                          for j in range(2)], axis=1).astype(BF16)
    kp = jnp.concatenate([_rope(prev_ref[:, ko + j * LANES:ko + (j + 1) * LANES], cosp, sinp)
                          for j in range(2)], axis=1).astype(BF16)
    kk = jnp.concatenate([kp, kc], axis=0)
    vw = A_KVHEADS * A_HEADDIM
    vv_t = jnp.concatenate([prev_ref[:, vo:vo + vw], cur_ref[:, vo:vo + vw]], axis=0).T
    krow = lax.broadcasted_iota(jnp.int32, (2 * L, L), 0)
    qcol = lax.broadcasted_iota(jnp.int32, (2 * L, L), 1)
    first_key = jnp.where(n > 0, 0, L)
    valid = jnp.logical_and(jnp.logical_and(krow > qcol, krow <= qcol + L), krow >= first_key)
    scale = A_HEADDIM ** -0.5
    for j in range(A_QHEADS // 2):
        qpair = _rope(cur_ref[:, qo + j * LANES:qo + (j + 1) * LANES], cos, sin).astype(BF16)
        outs = []
        for t in range(2):
            hq = 2 * j + t
            g = hq // A_REP
            sl = slice(g * A_HEADDIM, (g + 1) * A_HEADDIM)
            qh = qpair[:, t * A_HEADDIM:(t + 1) * A_HEADDIM]
            s_t = jnp.where(valid, _nt_dot(kk[:, sl], qh) * scale, NEG_INF)
            sink = sink_ref[:, hq:hq + 1]
            m = jnp.maximum(jnp.max(s_t, axis=0, keepdims=True), sink)
            p = jnp.exp(s_t - m)
            den = jnp.sum(p, axis=0, keepdims=True) + jnp.exp(sink - m)
            o_t = jnp.dot(vv_t[sl, :].astype(BF16), p.astype(BF16),
                          preferred_element_type=F32)
            outs.append(o_t / den)
        o_ref[:, j * LANES:(j + 1) * LANES] = jnp.concatenate(outs, axis=0).T.astype(o_ref.dtype)


def _swa(ap, cos_t, sin_t, sink_row, B, S):
    nb = S // CHUNK
    W = ap.shape[1]
    cur = lambda b, n: (b * nb + n, 0)
    prev = lambda b, n: (b * nb + jnp.maximum(n - 1, 0), 0)
    tab = lambda b, n: (n, 0)
    tabp = lambda b, n: (jnp.maximum(n - 1, 0), 0)
    return pl.pallas_call(
        _swa_kernel,
        grid=(B, nb),
        in_specs=[pl.BlockSpec((CHUNK, W), cur), pl.BlockSpec((CHUNK, W), prev),
                  pl.BlockSpec((CHUNK, LANES), tab), pl.BlockSpec((CHUNK, LANES), tab),
                  pl.BlockSpec((CHUNK, LANES), tabp), pl.BlockSpec((CHUNK, LANES), tabp),
                  pl.BlockSpec((1, LANES), lambda b, n: (0, 0))],
        out_specs=pl.BlockSpec((CHUNK, BRANCH_WIDTH), cur),
        out_shape=jax.ShapeDtypeStruct((B * S, BRANCH_WIDTH), BF16),
        compiler_params=_params("parallel", "parallel"),
        name="swa",
    )(ap, ap, cos_t, sin_t, cos_t, sin_t, sink_row)


def _merge_kernel(ym_ref, ys_ref, ya_ref, g0_ref, g1_ref, g2_ref, b0_ref, b1_ref, b2_ref,
                  w0_ref, w1_ref, w2_ref, o_ref):
    acc = None
    for y_ref, g_ref, b_ref, w_ref in ((ym_ref, g0_ref, b0_ref, w0_ref),
                                       (ys_ref, g1_ref, b1_ref, w1_ref),
                                       (ya_ref, g2_ref, b2_ref, w2_ref)):
        t = _sigmoid(g_ref[...] + b_ref[...]) * jnp.dot(y_ref[...], w_ref[0],
                                                        preferred_element_type=F32)
        acc = t if acc is None else acc + t
    o_ref[...] = acc.astype(o_ref.dtype)


def _merge(ym, ys, ya, gp, gb, wb, tm, tn):
    T = ym.shape[0]
    D = wb.shape[2]
    tm, tn = min(tm, T), min(tn, D)
    nj = D // tn
    yspec = pl.BlockSpec((tm, BRANCH_WIDTH), lambda i, j: (i, 0))
    gspec = lambda k: pl.BlockSpec((tm, tn), lambda i, j: (i, k * nj + j))
    bspec = lambda k: pl.BlockSpec((1, tn), lambda i, j: (0, k * nj + j))
    wspec = lambda k: pl.BlockSpec((1, BRANCH_WIDTH, tn), lambda i, j: (k, 0, j))
    return pl.pallas_call(
        _merge_kernel,
        grid=(T // tm, nj),
        in_specs=[yspec, yspec, yspec, gspec(0), gspec(1), gspec(2), bspec(0), bspec(1), bspec(2),
                  wspec(0), wspec(1), wspec(2)],
        out_specs=pl.BlockSpec((tm, tn), lambda i, j: (i, j)),
        out_shape=jax.ShapeDtypeStruct((T, D), BF16),
        compiler_params=_params("parallel", "arbitrary"),
        name="merge",
    )(ym, ys, ya, gp, gp, gp, gb, gb, gb, wb, wb, wb)


def _outproj_kernel(x_ref, mix_ref, w_ref, g_ref, b_ref, o_ref, ob_ref, *, alpha):
    y = alpha * x_ref[...] + jnp.dot(mix_ref[...], w_ref[...], preferred_element_type=F32)
    r = _layer_norm(y, g_ref[...], b_ref[...])
    o_ref[...] = r
    ob_ref[...] = r.astype(BF16)


def _outproj(x, mix, w, g, b, alpha, tm):
    T, D = x.shape
    tm = min(tm, T)
    rowspec = pl.BlockSpec((tm, D), lambda i: (i, 0))
    vec = pl.BlockSpec((1, D), lambda i: (0, 0))
    return pl.pallas_call(
        functools.partial(_outproj_kernel, alpha=alpha),
        grid=(T // tm,),
        in_specs=[rowspec, rowspec, pl.BlockSpec((D, D), lambda i: (0, 0)), vec, vec],
        out_specs=[rowspec, rowspec],
        out_shape=[jax.ShapeDtypeStruct((T, D), F32), jax.ShapeDtypeStruct((T, D), BF16)],
        compiler_params=_params("parallel"),
        name="outproj_ln",
    )(x, mix, w, g, b)


def _res_ln_kernel(x_ref, yt_ref, g_ref, b_ref, o_ref, ob_ref, *, alpha):
    r = _layer_norm(alpha * x_ref[...] + yt_ref[0].T, g_ref[...], b_ref[...])
    o_ref[...] = r
    ob_ref[...] = r.astype(BF16)


def _res_ln(x, yt, g, b, alpha):
    T, D = x.shape
    tm = yt.shape[2]
    rowspec = pl.BlockSpec((tm, D), lambda i: (i, 0))
    vec = pl.BlockSpec((1, D), lambda i: (0, 0))
    return pl.pallas_call(
        functools.partial(_res_ln_kernel, alpha=alpha),
        grid=(T // tm,),
        in_specs=[rowspec, pl.BlockSpec((1, D, tm), lambda i: (i, 0, 0)), vec, vec],
        out_specs=[rowspec, rowspec],
        out_shape=[jax.ShapeDtypeStruct((T, D), F32), jax.ShapeDtypeStruct((T, D), BF16)],
        compiler_params=_params("parallel"),
        name="res_ln",
    )(x, yt, g, b)


_CAND_NQ = (16, 8, 5, 4, 3, 2, 2, 2)
_CAND_ROWS = 16 + 7 * 8 + 8


def _extract_top(work, iota, n_iter, on_pick):
    big = float(work.shape[0])
    for p in range(n_iter):
        m = jnp.max(work, axis=0, keepdims=True)
        idx = jnp.min(jnp.where(work == m, iota, big), axis=0, keepdims=True)
        onehot = iota == idx
        work = jnp.where(onehot, NEG_INF, work)
        on_pick(p, m, onehot)


def _select_chunk(s1, s2, top_ref, cnt_ref, lanes):
    K = P_TOPK
    n = s1.shape[1]
    iota = lax.broadcasted_iota(jnp.int32, (P_NKEYS, n), 0).astype(F32)
    ranks = []
    for c, s in enumerate((s1, s2)):
        rank_holder = [jnp.full((P_NKEYS, n), float(K), F32)]

        def on_pick(p, m, onehot, c=c, rank_holder=rank_holder):
            top_ref[c, p:p + 1, lanes] = m
            rank_holder[0] = jnp.where(onehot, float(p), rank_holder[0])

        _extract_top(s, iota, K, on_pick)
        ranks.append(rank_holder[0])

    a = top_ref[0, :, lanes]
    b = top_ref[1, :, lanes]
    ea = jnp.exp(a - a[0:1])
    eb = jnp.exp(b - b[0:1])
    q8 = lax.broadcasted_iota(jnp.int32, (8, n), 0)
    cand = [a[0:1] + b]
    wcand = [ea[0:1] * eb]
    for p in range(1, 8):
        ok = q8 < _CAND_NQ[p]
        cand.append(jnp.where(ok, a[p:p + 1] + b[0:8], NEG_INF))
        wcand.append(ea[p:p + 1] * eb[0:8])
    cand.append(a[8:16] + b[0:1])
    wcand.append(ea[8:16] * eb[0:1])
    cand = jnp.concatenate(cand, axis=0)
    wcand = jnp.concatenate(wcand, axis=0)
    iota_c = lax.broadcasted_iota(jnp.int32, (_CAND_ROWS, n), 0).astype(F32)
    sel_holder = [jnp.zeros((_CAND_ROWS, n), F32)]

    def on_pick2(p, m, onehot):
        sel_holder[0] = jnp.where(onehot, 1.0, sel_holder[0])

    _extract_top(cand, iota_c, K, on_pick2)
    sel = sel_holder[0]
    z = jnp.sum(sel * wcand, axis=0, keepdims=True)
    cnt_ref[0:1, lanes] = jnp.sum(sel[0:16], axis=0, keepdims=True)
    for p in range(1, 8):
        cnt_ref[p:p + 1, lanes] = jnp.sum(sel[8 + 8 * p:16 + 8 * p], axis=0, keepdims=True)
    cnt_ref[8:16, lanes] = sel[72:80]
    cnt = cnt_ref[:, lanes]
    cnt1 = jnp.zeros((P_NKEYS, n), F32)
    for p in range(K):
        cnt1 = jnp.where(ranks[0] == float(p), cnt[p:p + 1], cnt1)
    return ranks[1], cnt1, jnp.exp(s1 - a[0:1]), jnp.exp(s2 - b[0:1]) / z


def _peer_select_kernel(xb_ref, wq_ref, sk_ref, rank2_ref, cnt1_ref, e1_ref, e2_ref,
                        q_scr, s_scr, top_scr, cnt_scr):
    h = pl.program_id(1)
    tm = xb_ref.shape[0]

    @pl.when(h == 0)
    def _():
        q = jnp.dot(xb_ref[...], wq_ref[...], preferred_element_type=F32)
        for hc in range(2 * P_HEADS):
            q_scr[hc] = q[:, hc * LANES:(hc + 1) * LANES]

    for c in range(2):
        s_scr[c] = _nt_dot(sk_ref[0, c], q_scr[2 * h + c], precision=HIGHEST)
    for lc in range(tm // LANES):
        lanes = slice(lc * LANES, (lc + 1) * LANES)
        rank2, cnt1, e1, e2 = _select_chunk(s_scr[0, :, lanes], s_scr[1, :, lanes],
                                            top_scr, cnt_scr, lanes)
        rank2_ref[0, :, lanes] = rank2
        cnt1_ref[0, :, lanes] = cnt1
        e1_ref[0, :, lanes] = e1
        e2_ref[0, :, lanes] = e2


def _peer_select(xb, wq, sk, tm):
    T, D = xb.shape
    tm = min(tm, T)
    out = jax.ShapeDtypeStruct((P_HEADS, P_NKEYS, T), F32)
    ospec = pl.BlockSpec((1, P_NKEYS, tm), lambda i, h: (h, 0, i))
    return pl.pallas_call(
        _peer_select_kernel,
        grid=(T // tm, P_HEADS),
        in_specs=[pl.BlockSpec((tm, D), lambda i, h: (i, 0)),
                  pl.BlockSpec(wq.shape, lambda i, h: (0, 0)),
                  pl.BlockSpec((1, 2, P_NKEYS, LANES), lambda i, h: (h, 0, 0, 0))],
        out_specs=[ospec, ospec, ospec, ospec],
        out_shape=[out, out, out, out],
        scratch_shapes=[pltpu.VMEM((2 * P_HEADS, tm, LANES), F32),
                        pltpu.VMEM((2, P_NKEYS, tm), F32),
                        pltpu.VMEM((2, P_TOPK, tm), F32),
                        pltpu.VMEM((P_TOPK, tm), F32)],
        compiler_params=_params("parallel", "arbitrary"),
        name="peer_select",
    )(xb, wq, sk)


def _peer_gated_act(act_t, blk, rank2_ref, cnt1_ref, e1_ref, e2_ref, nsub):
    parts = []
    for ii in range(nsub):
        i = blk * nsub + ii
        w = None
        for h in range(P_HEADS):
            cnt = cnt1_ref[h, pl.ds(i, 1), :]
            e1 = e1_ref[h, pl.ds(i, 1), :]
            t = jnp.where(rank2_ref[h] < cnt, e2_ref[h], 0.0) * e1
            w = t if w is None else w + t
        a = act_t[ii * P_NKEYS:(ii + 1) * P_NKEYS]
        gelu = 0.5 * a * (1.0 + lax.erf(a * (0.5 ** 0.5)))
        parts.append((w * gelu).astype(BF16))
    return jnp.concatenate(parts, axis=0) if nsub > 1 else parts[0]


def _peer_dense_kernel(xb_ref, rank2_ref, cnt1_ref, e1_ref, e2_ref, u_ref, vta_ref, vtb_ref, o_ref,
                       pa_ref, pb_ref, *, nsub):
    s = pl.program_id(1)
    ns = pl.num_programs(1) - 1
    be = nsub * P_NKEYS
    sel = (rank2_ref, cnt1_ref, e1_ref, e2_ref)

    @pl.when(s == 0)
    def _():
        o_ref[0] = jnp.zeros(o_ref.shape[1:], F32)
        pa_ref[...] = jnp.zeros_like(pa_ref)

    @pl.when(s < ns)
    def _():
        xb = xb_ref[...]
        act0 = _nt_dot(u_ref[0:be, :], xb)
        act1 = _nt_dot(u_ref[be:2 * be, :], xb)
        o_ref[0] += jnp.dot(vta_ref[...], pa_ref[...], preferred_element_type=F32)
        pb_ref[...] = _peer_gated_act(act0, 2 * s, *sel, nsub)
        o_ref[0] += jnp.dot(vtb_ref[...], pb_ref[...], preferred_element_type=F32)
        pa_ref[...] = _peer_gated_act(act1, 2 * s + 1, *sel, nsub)

    @pl.when(s == ns)
    def _():
        o_ref[0] += jnp.dot(vta_ref[...], pa_ref[...], preferred_element_type=F32)


def _peer_dense(xb, rank2, cnt1, e1, e2, u_b, vt_b, tm, be):
    T, D = xb.shape
    tm = min(tm, T)
    nsub = be // P_NKEYS
    nblk = P_EXPERTS // be
    ns = nblk // 2
    sel = pl.BlockSpec((P_HEADS, P_NKEYS, tm), lambda i, s: (0, 0, i))
    return pl.pallas_call(
        functools.partial(_peer_dense_kernel, nsub=nsub),
        grid=(T // tm, ns + 1),
        in_specs=[pl.BlockSpec((tm, D), lambda i, s: (i, 0)), sel, sel, sel, sel,
                  pl.BlockSpec((2 * be, D), lambda i, s: (jnp.minimum(s, ns - 1), 0)),
                  pl.BlockSpec((D, be), lambda i, s: (0, jnp.maximum(2 * s - 1, 0))),
                  pl.BlockSpec((D, be), lambda i, s: (0, jnp.minimum(2 * s, nblk - 1)))],
        out_specs=pl.BlockSpec((1, D, tm), lambda i, s: (i, 0, 0)),
        out_shape=jax.ShapeDtypeStruct((T // tm, D, tm), F32),
        scratch_shapes=[pltpu.VMEM((be, tm), BF16), pltpu.VMEM((be, tm), BF16)],
        compiler_params=_params("parallel", "arbitrary"),
        name="peer_dense",
    )(xb, rank2, cnt1, e1, e2, u_b, vt_b, vt_b)


def _pad_row(vals_at, width=LANES):
    row = jnp.zeros((1, width), F32)
    for off, v in vals_at:
        row = row.at[0, off:off + v.shape[0]].set(v.astype(F32))
    return row


def _rope_tables(S):
    half = A_HEADDIM // 2
    freqs = ROPE_THETA ** (-jnp.arange(half, dtype=F32) / half)
    ang = jnp.arange(S, dtype=F32)[:, None] * freqs[None, :]
    reps = LANES // half
    return jnp.tile(jnp.cos(ang), (1, reps)), jnp.tile(jnp.sin(ang), (1, reps))


def _head_expand_matrix():
    e = np.zeros((S_GROUPS, LANES, S_HPG * S_HEADDIM), np.float32)
    for g in range(S_GROUPS):
        for hh in range(S_HPG):
            e[g, SM_DT + g * S_HPG + hh, hh * S_HEADDIM:(hh + 1) * S_HEADDIM] = 1.0
    return jnp.asarray(e)


def _layer(x, xb, p, consts, B, S, alpha):
    T, D = x.shape
    w_in = p["w_in"]
    o_m = 2 * M_HEADS * M_DQK + 2 * M_HEADS * M_DV
    o_s = o_m + 2 * M_HEADS
    o_dt = o_s + S_DINNER + S_CONV_CH
    o_a = o_dt + S_HEADS
    o_g = o_a + (A_QHEADS + 2 * A_KVHEADS) * A_HEADDIM
    w_m = w_in[:, :o_m].astype(BF16)
    w_s = w_in[:, o_s:o_dt].astype(BF16)
    w_a = w_in[:, o_a:o_g].astype(BF16)
    w_g = w_in[:, o_g:].astype(BF16)
    w_sm = jnp.concatenate([w_in[:, o_m:o_s], w_in[:, o_dt:o_a],
                            jnp.zeros((D, LANES - 2 * M_HEADS - S_HEADS), F32)], axis=1).astype(BF16)

    mp = _matmul(xb, w_m, 1024, 512)
    sp = _matmul(xb, w_s, 1024, 512)
    ap = _matmul(xb, w_a, 1024, 512)
    gp = _matmul(xb, w_g, 1024, 512)
    sm = _matmul(xb, w_sm, 1024, LANES)

    brow = _pad_row([(SM_I, p["mlstm_gate_b"][0]), (SM_F, p["mlstm_gate_b"][1]),
                     (SM_DT, p["ssm_dt_bias"])])
    arow = _pad_row([(SM_DT, p["ssm_a_log"])])
    drow = jnp.repeat(p["ssm_d"].astype(F32), S_HEADDIM)[None, :]
    y_m = _mlstm(mp, sm, brow, p["mlstm_norm_w"][None, :], B, S)
    y_s = _ssd(sp, sm, brow, arow, drow, p["ssm_conv_w"][:, 0, :], p["ssm_conv_b"][None, :],
               p["ssm_norm_w"][None, :], consts["e_mat"], B, S)
    y_a = _swa(ap, consts["cos"], consts["sin"], _pad_row([(0, p["swa_sinks"])]), B, S)

    mix = _merge(y_m, y_s, y_a, gp, p["merge_gate_b"].reshape(1, 3 * D),
                 p["w_branch"].astype(BF16), 1024, 512)
    x1, x1b = _outproj(x, mix, p["w_out"].astype(BF16), p["ln1_g"][None, :], p["ln1_b"][None, :],
                       alpha, 512)

    rank2, cnt1, e1, e2 = _peer_select(x1b, p["peer_wq"].astype(BF16), p["peer_subkeys"], 512)
    peer = _peer_dense(x1b, rank2, cnt1, e1, e2, p["peer_u"].astype(BF16),
                       p["peer_v"].T.astype(BF16), 512, 256)
    return _res_ln(x1, peer, p["ln2_g"][None, :], p["ln2_b"][None, :], alpha)


def _forward(x, params, depth):
    B, S, D = x.shape
    alpha = (2.0 * depth) ** 0.25
    cos_t, sin_t = _rope_tables(S)
    consts = {"cos": cos_t, "sin": sin_t, "e_mat": _head_expand_matrix()}
    xf = x.reshape(B * S, D)

    def body(carry, p):
        xc, xcb = carry
        return _layer(xc, xcb, p, consts, B, S, alpha), None

    (xf, _), _ = lax.scan(body, (xf, xf.astype(BF16)), params)
    return xf.reshape(B, S, D)


def kernel(x, w_in, mlstm_gate_b, mlstm_norm_w, ssm_conv_w, ssm_conv_b, ssm_dt_bias, ssm_a_log,
           ssm_d, ssm_norm_w, swa_sinks, merge_gate_b, w_branch, w_out, ln1_g, ln1_b,
           peer_wq, peer_subkeys, peer_u, peer_v, ln2_g, ln2_b):
    params = dict(w_in=w_in, mlstm_gate_b=mlstm_gate_b, mlstm_norm_w=mlstm_norm_w,
                  ssm_conv_w=ssm_conv_w, ssm_conv_b=ssm_conv_b, ssm_dt_bias=ssm_dt_bias,
                  ssm_a_log=ssm_a_log, ssm_d=ssm_d, ssm_norm_w=ssm_norm_w, swa_sinks=swa_sinks,
                  merge_gate_b=merge_gate_b, w_branch=w_branch, w_out=w_out, ln1_g=ln1_g,
                  ln1_b=ln1_b, peer_wq=peer_wq, peer_subkeys=peer_subkeys, peer_u=peer_u,
                  peer_v=peer_v, ln2_g=ln2_g, ln2_b=ln2_b)
    return _forward(x, params, w_in.shape[0])
```

```python
import functools
import math

import jax
import jax.numpy as jnp
import numpy as np
from jax import lax
from jax.experimental import pallas as pl
from jax.experimental.pallas import tpu as pltpu

F32 = jnp.float32
BF16 = jnp.bfloat16
HIGHEST = lax.Precision.HIGHEST

LN_EPS = 1e-5
NORM_EPS = 1e-6

LANES = 128
CHUNK = 128
VMEM_LIMIT = 56 * 1024 * 1024

M_HEADS, M_DQK, M_DV = 4, 128, 256
S_HEADS, S_GROUPS, S_HPG, S_HEADDIM, S_DSTATE, S_CONV = 16, 2, 8, 64, 128, 4
S_DINNER = 1024
S_CONV_CH = S_DINNER + 2 * S_GROUPS * S_DSTATE
A_QHEADS, A_KVHEADS, A_HEADDIM, A_REP = 16, 4, 64, 4
ROPE_THETA = 10000.0
P_HEADS, P_NKEYS, P_TOPK = 8, 128, 16
P_EXPERTS = P_NKEYS * P_NKEYS
BRANCH_WIDTH = 1024

SM_I, SM_F, SM_DT = 0, M_HEADS, 2 * M_HEADS

NEG_INF = float("-inf")


def _params(*sem, flags=None):
    return pltpu.CompilerParams(dimension_semantics=sem, vmem_limit_bytes=VMEM_LIMIT, flags=flags)


def _nt_dot(a, b, **kw):
    return lax.dot_general(a, b, (((1,), (1,)), ((), ())), preferred_element_type=F32, **kw)


def _softplus(x):
    return jnp.maximum(x, 0.0) + jnp.log(1.0 + jnp.exp(-jnp.abs(x)))


def _log_sigmoid(x):
    return jnp.minimum(x, 0.0) - jnp.log(1.0 + jnp.exp(-jnp.abs(x)))


def _sigmoid(x):
    return 1.0 / (1.0 + jnp.exp(-x))


def _silu(x):
    return x * _sigmoid(x)


def _layer_norm(y, g, b):
    mu = jnp.mean(y, -1, keepdims=True)
    d = y - mu
    var = jnp.mean(d * d, -1, keepdims=True)
    return d * lax.rsqrt(var + LN_EPS) * g + b


def _mm_kernel(x_ref, w_ref, o_ref):
    o_ref[...] = jnp.dot(x_ref[...], w_ref[...], preferred_element_type=F32).astype(o_ref.dtype)


def _matmul(x, w, tm, tn, out_dtype=F32):
    T, K = x.shape
    N = w.shape[1]
    tm, tn = min(tm, T), min(tn, N)
    return pl.pallas_call(
        _mm_kernel,
        grid=(T // tm, N // tn),
        in_specs=[pl.BlockSpec((tm, K), lambda i, j: (i, 0)),
                  pl.BlockSpec((K, tn), lambda i, j: (0, j))],
        out_specs=pl.BlockSpec((tm, tn), lambda i, j: (i, j)),
        out_shape=jax.ShapeDtypeStruct((T, N), out_dtype),
        compiler_params=_params("parallel", "arbitrary"),
        name="proj_matmul",
    )(x, w)


def _mlstm_kernel(mp_ref, sm_ref, brow_ref, nw_ref, o_ref, c_ref, n_ref, m_ref):
    L = CHUNK

    @pl.when(pl.program_id(1) == 0)
    def _():
        c_ref[...] = jnp.zeros_like(c_ref)
        n_ref[...] = jnp.zeros_like(n_ref)
        m_ref[...] = jnp.zeros_like(m_ref)

    gates = sm_ref[...] + brow_ref[...]
    row = lax.broadcasted_iota(jnp.int32, (L, L), 0)
    col = lax.broadcasted_iota(jnp.int32, (L, L), 1)
    tril = row >= col
    trilf = tril.astype(F32)
    scale = M_DQK ** -0.5
    for h in range(M_HEADS):
        q = mp_ref[:, h * M_DQK:(h + 1) * M_DQK]
        k = mp_ref[:, 512 + h * M_DQK:512 + (h + 1) * M_DQK] * scale
        v = mp_ref[:, 1024 + h * M_DV:1024 + (h + 1) * M_DV].astype(BF16)
        og = mp_ref[:, 2048 + h * M_DV:2048 + (h + 1) * M_DV]
        li_col = gates[:, SM_I + h:SM_I + h + 1]
        lf_col = _log_sigmoid(gates[:, SM_F + h:SM_F + h + 1])
        bc = jnp.dot(trilf, jnp.broadcast_to(lf_col, (L, L)), precision=HIGHEST,
                     preferred_element_type=F32)
        br = bc.T
        lir = jnp.broadcast_to(li_col, (L, L)).T
        b_col = bc[:, 0:1]
        m_prev = m_ref[h, 0:1, 0:1]
        log_d = jnp.where(tril, bc - br + lir, NEG_INF)
        m_t = jnp.maximum(jnp.max(log_d, axis=1, keepdims=True), b_col + m_prev)
        w_intra = jnp.exp(log_d - m_t)
        w_inter = jnp.exp(b_col + m_prev - m_t)
        qb = q.astype(BF16)
        s = _nt_dot(qb, k.astype(BF16)) * w_intra
        c_old = c_ref[h]
        n_old = n_ref[h, 0:1, :]
        num = (jnp.dot(s.astype(BF16), v, preferred_element_type=F32)
               + w_inter * jnp.dot(qb, c_old.astype(BF16), preferred_element_type=F32))
        den = (jnp.sum(s, axis=1, keepdims=True)
               + w_inter * jnp.sum(q * n_old, axis=1, keepdims=True))
        hh = num / jnp.maximum(jnp.abs(den), jnp.exp(-m_t))

        g = bc[L - 1:L, 0:1]
        lws = g - b_col + li_col
        m_new = jnp.maximum(g + m_prev, jnp.max(lws, axis=0, keepdims=True))
        a_prev = jnp.exp(g + m_prev - m_new)
        kw = k * jnp.exp(lws - m_new)
        c_ref[h] = a_prev * c_old + jnp.dot(kw.T.astype(BF16), v, preferred_element_type=F32)
        n_ref[h, 0:1, :] = a_prev * n_old + jnp.sum(kw, axis=0, keepdims=True)
        m_ref[h] = jnp.broadcast_to(m_new, m_ref.shape[1:])

        mu = jnp.mean(hh, -1, keepdims=True)
        d = hh - mu
        var = jnp.mean(d * d, -1, keepdims=True)
        hn = d * lax.rsqrt(var + NORM_EPS) * nw_ref[:, h * M_DV:(h + 1) * M_DV]
        o_ref[:, h * M_DV:(h + 1) * M_DV] = (_sigmoid(og) * hn).astype(o_ref.dtype)


def _mlstm(mp, sm, brow, nw, B, S):
    nc = S // CHUNK
    W = mp.shape[1]
    return pl.pallas_call(
        _mlstm_kernel,
        grid=(B, nc),
        in_specs=[pl.BlockSpec((CHUNK, W), lambda b, c: (b * nc + c, 0)),
                  pl.BlockSpec((CHUNK, LANES), lambda b, c: (b * nc + c, 0)),
                  pl.BlockSpec((1, LANES), lambda b, c: (0, 0)),
                  pl.BlockSpec((1, BRANCH_WIDTH), lambda b, c: (0, 0))],
        out_specs=pl.BlockSpec((CHUNK, BRANCH_WIDTH), lambda b, c: (b * nc + c, 0)),
        out_shape=jax.ShapeDtypeStruct((B * S, BRANCH_WIDTH), BF16),
        scratch_shapes=[pltpu.VMEM((M_HEADS, M_DQK, M_DV), F32),
                        pltpu.VMEM((M_HEADS, 8, M_DQK), F32),
                        pltpu.VMEM((M_HEADS, 8, LANES), F32)],
        compiler_params=_params("parallel", "arbitrary"),
        name="mlstm",
    )(mp, sm, brow, nw)


def _ssd_kernel(cur_ref, prev_ref, sm_ref, brow_ref, arow_ref, drow_ref, cw_ref, cb_ref, nw_ref,
                e_ref, o_ref, st_ref):
    L = CHUNK
    c = pl.program_id(1)

    @pl.when(c == 0)
    def _():
        st_ref[...] = jnp.zeros_like(st_ref)

    u = cur_ref[:, S_DINNER:]
    up = prev_ref[:, S_DINNER:] * (c > 0).astype(F32)
    rows = lax.broadcasted_iota(jnp.int32, (L, S_CONV_CH), 0)
    conv = cb_ref[...] + cw_ref[S_CONV - 1:S_CONV, :] * u
    for kk in range(S_CONV - 1):
        sh = S_CONV - 1 - kk
        shifted = jnp.where(rows < sh, pltpu.roll(up, sh, 0), pltpu.roll(u, sh, 0))
        conv = conv + cw_ref[kk:kk + 1, :] * shifted
    xbc = _silu(conv)

    dt = _softplus(sm_ref[...] + brow_ref[...])
    d_a = dt * (-jnp.exp(arow_ref[...]))
    row = lax.broadcasted_iota(jnp.int32, (L, L), 0)
    col = lax.broadcasted_iota(jnp.int32, (L, L), 1)
    tril = row >= col
    acum = jnp.dot(tril.astype(F32), d_a, precision=HIGHEST, preferred_element_type=F32)

    gw = S_HPG * S_HEADDIM
    for g in range(S_GROUPS):
        xg = xbc[:, g * gw:(g + 1) * gw]
        bg = xbc[:, S_DINNER + g * S_DSTATE:S_DINNER + (g + 1) * S_DSTATE]
        cg = xbc[:, S_DINNER + (S_GROUPS + g) * S_DSTATE:S_DINNER + (S_GROUPS + g + 1) * S_DSTATE]
        cgb = cg.astype(BF16)
        cb = _nt_dot(cgb, bg.astype(BF16))
        e_g = e_ref[g]
        dt_e = jnp.dot(dt, e_g, precision=HIGHEST, preferred_element_type=F32)
        acum_e = jnp.dot(acum, e_g, precision=HIGHEST, preferred_element_type=F32)
        xc = xg * dt_e
        xcb = xc.astype(BF16)
        last = acum_e[L - 1:L, :]
        ys = []
        for hh in range(S_HPG):
            lane = SM_DT + g * S_HPG + hh
            ac = jnp.broadcast_to(acum[:, lane:lane + 1], (L, L))
            dec = jnp.exp(jnp.where(tril, ac - ac.T, NEG_INF))
            ys.append(jnp.dot((cb * dec).astype(BF16), xcb[:, hh * S_HEADDIM:(hh + 1) * S_HEADDIM],
                              preferred_element_type=F32))
        y = jnp.concatenate(ys, axis=1)
        st = st_ref[g]
        y = y + jnp.dot(cgb, st.astype(BF16), preferred_element_type=F32) * jnp.exp(acum_e)
        upd = jnp.dot(bg.T.astype(BF16), (jnp.exp(last - acum_e) * xc).astype(BF16),
                      preferred_element_type=F32)
        st_ref[g] = jnp.exp(last) * st + upd
        y = y + drow_ref[:, g * gw:(g + 1) * gw] * xg
        y = y * _silu(cur_ref[:, g * gw:(g + 1) * gw])
        y = y * lax.rsqrt(jnp.mean(y * y, -1, keepdims=True) + NORM_EPS)
        o_ref[:, g * gw:(g + 1) * gw] = (y * nw_ref[:, g * gw:(g + 1) * gw]).astype(o_ref.dtype)


def _ssd(sp, sm, brow, arow, drow, cw, cb, nw, e_mat, B, S):
    nc = S // CHUNK
    W = sp.shape[1]
    const = lambda b, c: (0, 0)
    return pl.pallas_call(
        _ssd_kernel,
        grid=(B, nc),
        in_specs=[pl.BlockSpec((CHUNK, W), lambda b, c: (b * nc + c, 0)),
                  pl.BlockSpec((CHUNK, W), lambda b, c: (b * nc + jnp.maximum(c - 1, 0), 0)),
                  pl.BlockSpec((CHUNK, LANES), lambda b, c: (b * nc + c, 0)),
                  pl.BlockSpec((1, LANES), const),
                  pl.BlockSpec((1, LANES), const),
                  pl.BlockSpec((1, S_DINNER), const),
                  pl.BlockSpec((S_CONV, S_CONV_CH), const),
                  pl.BlockSpec((1, S_CONV_CH), const),
                  pl.BlockSpec((1, S_DINNER), const),
                  pl.BlockSpec((S_GROUPS, LANES, S_HPG * S_HEADDIM), lambda b, c: (0, 0, 0))],
        out_specs=pl.BlockSpec((CHUNK, S_DINNER), lambda b, c: (b * nc + c, 0)),
        out_shape=jax.ShapeDtypeStruct((B * S, S_DINNER), BF16),
        scratch_shapes=[pltpu.VMEM((S_GROUPS, S_DSTATE, S_HPG * S_HEADDIM), F32)],
        compiler_params=_params("parallel", "arbitrary"),
        name="ssd",
    )(sp, sp, sm, brow, arow, drow, cw, cb, nw, e_mat)


def _rope(x, cos, sin):
    lane = lax.broadcasted_iota(jnp.int32, x.shape, 1)
    first_half = (lane % A_HEADDIM) < (A_HEADDIM // 2)
    rot = jnp.where(first_half, -pltpu.roll(x, LANES - A_HEADDIM // 2, 1),
                    pltpu.roll(x, A_HEADDIM // 2, 1))
    return x * cos + rot * sin


def _swa_kernel(cur_ref, prev_ref, cos_ref, sin_ref, cosp_ref, sinp_ref, sink_ref, o_ref):
    L = CHUNK
    n = pl.program_id(1)
    cos, sin = cos_ref[...], sin_ref[...]
    cosp, sinp = cosp_ref[...], sinp_ref[...]
    qo, ko, vo = 0, A_QHEADS * A_HEADDIM, (A_QHEADS + A_KVHEADS) * A_HEADDIM
    kc = jnp.concatenate([_rope(cur_ref[:, ko + j * LANES:ko + (j + 1) * LANES], cos, sin)
                          for j in range(2)], axis=1).astype(BF16)
    kp = jnp.concatenate([_rope(prev_ref[:, ko + j * LANES:ko + (j + 1) * LANES], cosp, sinp)
                          for j in range(2)], axis=1).astype(BF16)
    kk = jnp.concatenate([kp, kc], axis=0)
    vw = A_KVHEADS * A_HEADDIM
    vv_t = jnp.concatenate([prev_ref[:, vo:vo + vw], cur_ref[:, vo:vo + vw]], axis=0).T
    krow = lax.broadcasted_iota(jnp.int32, (2 * L, L), 0)
    qcol = lax.broadcasted_iota(jnp.int32, (2 * L, L), 1)
    first_key = jnp.where(n > 0, 0, L)
    valid = jnp.logical_and(jnp.logical_and(krow > qcol, krow <= qcol + L), krow >= first_key)
    scale = A_HEADDIM ** -0.5
    for j in range(A_QHEADS // 2):
        qpair = _rope(cur_ref[:, qo + j * LANES:qo + (j + 1) * LANES], cos, sin).astype(BF16)
        outs = []
        for t in range(2):
            hq = 2 * j + t
            g = hq // A_REP
            sl = slice(g * A_HEADDIM, (g + 1) * A_HEADDIM)
            qh = qpair[:, t * A_HEADDIM:(t + 1) * A_HEADDIM]
            s_t = jnp.where(valid, _nt_dot(kk[:, sl], qh) * scale, NEG_INF)
            sink = sink_ref[:, hq:hq + 1]
            m = jnp.maximum(jnp.max(s_t, axis=0, keepdims=True), sink)
            p = jnp.exp(s_t - m)
            den = jnp.sum(p, axis=0, keepdims=True) + jnp.exp(sink - m)
            o_t = jnp.dot(vv_t[sl, :].astype(BF16), p.astype(BF16),
                          preferred_element_type=F32)
            outs.append(o_t / den)
        o_ref[:, j * LANES:(j + 1) * LANES] = jnp.concatenate(outs, axis=0).T.astype(o_ref.dtype)


def _swa(ap, cos_t, sin_t, sink_row, B, S):
    nb = S // CHUNK
    W = ap.shape[1]
    cur = lambda b, n: (b * nb + n, 0)
    prev = lambda b, n: (b * nb + jnp.maximum(n - 1, 0), 0)
    tab = lambda b, n: (n, 0)
    tabp = lambda b, n: (jnp.maximum(n - 1, 0), 0)
    return pl.pallas_call(
        _swa_kernel,
        grid=(B, nb),
        in_specs=[pl.BlockSpec((CHUNK, W), cur), pl.BlockSpec((CHUNK, W), prev),
                  pl.BlockSpec((CHUNK, LANES), tab), pl.BlockSpec((CHUNK, LANES), tab),
                  pl.BlockSpec((CHUNK, LANES), tabp), pl.BlockSpec((CHUNK, LANES), tabp),
                  pl.BlockSpec((1, LANES), lambda b, n: (0, 0))],
        out_specs=pl.BlockSpec((CHUNK, BRANCH_WIDTH), cur),
        out_shape=jax.ShapeDtypeStruct((B * S, BRANCH_WIDTH), BF16),
        compiler_params=_params("parallel", "parallel"),
        name="swa",
    )(ap, ap, cos_t, sin_t, cos_t, sin_t, sink_row)


def _merge_kernel(ym_ref, ys_ref, ya_ref, g0_ref, g1_ref, g2_ref, b0_ref, b1_ref, b2_ref,
                  w0_ref, w1_ref, w2_ref, o_ref):
    acc = None
    for y_ref, g_ref, b_ref, w_ref in ((ym_ref, g0_ref, b0_ref, w0_ref),
                                       (ys_ref, g1_ref, b1_ref, w1_ref),
                                       (ya_ref, g2_ref, b2_ref, w2_ref)):
        t = _sigmoid(g_ref[...] + b_ref[...]) * jnp.dot(y_ref[...], w_ref[0],
                                                        preferred_element_type=F32)
        acc = t if acc is None else acc + t
    o_ref[...] = acc.astype(o_ref.dtype)


def _merge(ym, ys, ya, gp, gb, wb, tm, tn):
    T = ym.shape[0]
    D = wb.shape[2]
    tm, tn = min(tm, T), min(tn, D)
    nj = D // tn
    yspec = pl.BlockSpec((tm, BRANCH_WIDTH), lambda i, j: (i, 0))
    gspec = lambda k: pl.BlockSpec((tm, tn), lambda i, j: (i, k * nj + j))
    bspec = lambda k: pl.BlockSpec((1, tn), lambda i, j: (0, k * nj + j))
    wspec = lambda k: pl.BlockSpec((1, BRANCH_WIDTH, tn), lambda i, j: (k, 0, j))
    return pl.pallas_call(
        _merge_kernel,
        grid=(T // tm, nj),
        in_specs=[yspec, yspec, yspec, gspec(0), gspec(1), gspec(2), bspec(0), bspec(1), bspec(2),
                  wspec(0), wspec(1), wspec(2)],
        out_specs=pl.BlockSpec((tm, tn), lambda i, j: (i, j)),
        out_shape=jax.ShapeDtypeStruct((T, D), BF16),
        compiler_params=_params("parallel", "arbitrary"),
        name="merge",
    )(ym, ys, ya, gp, gp, gp, gb, gb, gb, wb, wb, wb)


def _outproj_kernel(x_ref, mix_ref, w_ref, g_ref, b_ref, o_ref, ob_ref, *, alpha):
    y = alpha * x_ref[...] + jnp.dot(mix_ref[...], w_ref[...], preferred_element_type=F32)
    r = _layer_norm(y, g_ref[...], b_ref[...])
    o_ref[...] = r
    ob_ref[...] = r.astype(BF16)


def _outproj(x, mix, w, g, b, alpha, tm):
    T, D = x.shape
    tm = min(tm, T)
    rowspec = pl.BlockSpec((tm, D), lambda i: (i, 0))
    vec = pl.BlockSpec((1, D), lambda i: (0, 0))
    return pl.pallas_call(
        functools.partial(_outproj_kernel, alpha=alpha),
        grid=(T // tm,),
        in_specs=[rowspec, rowspec, pl.BlockSpec((D, D), lambda i: (0, 0)), vec, vec],
        out_specs=[rowspec, rowspec],
        out_shape=[jax.ShapeDtypeStruct((T, D), F32), jax.ShapeDtypeStruct((T, D), BF16)],
        compiler_params=_params("parallel"),
        name="outproj_ln",
    )(x, mix, w, g, b)


def _res_ln_kernel(x_ref, yt_ref, g_ref, b_ref, o_ref, ob_ref, *, alpha):
    r = _layer_norm(alpha * x_ref[...] + yt_ref[0].T, g_ref[...], b_ref[...])
    o_ref[...] = r
    ob_ref[...] = r.astype(BF16)


def _res_ln(x, yt, g, b, alpha):
    T, D = x.shape
    tm = yt.shape[2]
    rowspec = pl.BlockSpec((tm, D), lambda i: (i, 0))
    vec = pl.BlockSpec((1, D), lambda i: (0, 0))
    return pl.pallas_call(
        functools.partial(_res_ln_kernel, alpha=alpha),
        grid=(T // tm,),
        in_specs=[rowspec, pl.BlockSpec((1, D, tm), lambda i: (i, 0, 0)), vec, vec],
        out_specs=[rowspec, rowspec],
        out_shape=[jax.ShapeDtypeStruct((T, D), F32), jax.ShapeDtypeStruct((T, D), BF16)],
        compiler_params=_params("parallel"),
        name="res_ln",
    )(x, yt, g, b)


_CAND_NQ = (16, 8, 5, 4, 3, 2, 2, 2)
_CAND_ROWS = 16 + 7 * 8 + 8


def _extract_top(work, iota, n_iter, on_pick):
    big = float(work.shape[0])
    for p in range(n_iter):
        m = jnp.max(work, axis=0, keepdims=True)
        idx = jnp.min(jnp.where(work == m, iota, big), axis=0, keepdims=True)
        onehot = iota == idx
        work = jnp.where(onehot, NEG_INF, work)
        on_pick(p, m, onehot)


def _select_chunk(s1, s2, top_ref, cnt_ref, lanes, exact):
    K = P_TOPK
    n = s1.shape[1]
    iota = lax.broadcasted_iota(jnp.int32, (P_NKEYS, n), 0).astype(F32)
    ranks = []
    suspect = jnp.zeros((1, n), F32)
    for c, s in enumerate((s1, s2)):
        rank = jnp.full((P_NKEYS, n), float(K), F32)
        if exact:
            rank_holder = [rank]

            def on_pick(p, m, onehot, c=c, rank_holder=rank_holder):
                top_ref[c, p:p + 1, lanes] = m
                rank_holder[0] = jnp.where(onehot, float(p), rank_holder[0])

            _extract_top(s, iota, K, on_pick)
            rank = rank_holder[0]
        else:
            work = s
            for p in range(K):
                m = jnp.max(work, axis=0, keepdims=True)
                hit = work == m
                work = jnp.where(hit, NEG_INF, work)
                rank = jnp.where(hit, float(p), rank)
                top_ref[c, p:p + 1, lanes] = m
            n_ranked = jnp.sum(jnp.where(rank < float(K), 1.0, 0.0), axis=0, keepdims=True)
            suspect = suspect + jnp.abs(n_ranked - float(K))
        ranks.append(rank)

    a = top_ref[0, :, lanes]
    b = top_ref[1, :, lanes]
    ea = jnp.exp(a - a[0:1])
    eb = jnp.exp(b - b[0:1])
    q8 = lax.broadcasted_iota(jnp.int32, (8, n), 0)
    cand = [a[0:1] + b]
    wcand = [ea[0:1] * eb]
    for p in range(1, 8):
        ok = q8 < _CAND_NQ[p]
        cand.append(jnp.where(ok, a[p:p + 1] + b[0:8], NEG_INF))
        wcand.append(ea[p:p + 1] * eb[0:8])
    cand.append(a[8:16] + b[0:1])
    wcand.append(ea[8:16] * eb[0:1])
    cand = jnp.concatenate(cand, axis=0)
    wcand = jnp.concatenate(wcand, axis=0)
    iota_c = lax.broadcasted_iota(jnp.int32, (_CAND_ROWS, n), 0).astype(F32)
    sel_holder = [jnp.zeros((_CAND_ROWS, n), F32)]

    def on_pick2(p, m, onehot):
        sel_holder[0] = jnp.where(onehot, 1.0, sel_holder[0])

    _extract_top(cand, iota_c, K, on_pick2)
    sel = sel_holder[0]
    z = jnp.sum(sel * wcand, axis=0, keepdims=True)
    cnt_ref[0:1, lanes] = jnp.sum(sel[0:16], axis=0, keepdims=True)
    for p in range(1, 8):
        cnt_ref[p:p + 1, lanes] = jnp.sum(sel[8 + 8 * p:16 + 8 * p], axis=0, keepdims=True)
    cnt_ref[8:16, lanes] = sel[72:80]
    cnt = cnt_ref[:, lanes]
    cnt1 = jnp.zeros((P_NKEYS, n), F32)
    for p in range(K):
        cnt1 = jnp.where(ranks[0] == float(p), cnt[p:p + 1], cnt1)
    return ranks[1], cnt1, jnp.exp(s1 - a[0:1]), jnp.exp(s2 - b[0:1]) / z, suspect


def _peer_select_kernel(xb_ref, wq_ref, sk_ref, rank2_ref, cnt1_ref, e1_ref, e2_ref,
                        q_scr, s_scr, top_scr, cnt_scr):
    h = pl.program_id(1)
    tm = xb_ref.shape[0]

    @pl.when(h == 0)
    def _():
        q = jnp.dot(xb_ref[...], wq_ref[...], preferred_element_type=F32)
        for hc in range(2 * P_HEADS):
            q_scr[hc] = q[:, hc * LANES:(hc + 1) * LANES]

    for c in range(2):
        s_scr[c] = _nt_dot(sk_ref[0, c], q_scr[2 * h + c], precision=HIGHEST)

    def select_tile(exact):
        suspect = jnp.zeros((1, LANES), F32)
        for lc in range(tm // LANES):
            lanes = slice(lc * LANES, (lc + 1) * LANES)
            rank2, cnt1, e1, e2, sus = _select_chunk(s_scr[0, :, lanes], s_scr[1, :, lanes],
                                                     top_scr, cnt_scr, lanes, exact)
            rank2_ref[0, :, lanes] = rank2.astype(rank2_ref.dtype)
            cnt1_ref[0, :, lanes] = cnt1
            e1_ref[0, :, lanes] = e1
            e2_ref[0, :, lanes] = e2.astype(e2_ref.dtype)
            suspect = suspect + sus
        return suspect

    suspect = select_tile(exact=False)

    @pl.when(jnp.max(suspect) > 0.0)
    def _():
        select_tile(exact=True)


def _peer_select(xb, wq, sk, tm):
    T, D = xb.shape
    tm = min(tm, T)
    out = jax.ShapeDtypeStruct((P_HEADS, P_NKEYS, T), F32)
    outb = jax.ShapeDtypeStruct((P_HEADS, P_NKEYS, T), BF16)
    ospec = pl.BlockSpec((1, P_NKEYS, tm), lambda i, h: (h, 0, i))
    return pl.pallas_call(
        _peer_select_kernel,
        grid=(T // tm, P_HEADS),
        in_specs=[pl.BlockSpec((tm, D), lambda i, h: (i, 0)),
                  pl.BlockSpec(wq.shape, lambda i, h: (0, 0)),
                  pl.BlockSpec((1, 2, P_NKEYS, LANES), lambda i, h: (h, 0, 0, 0))],
        out_specs=[ospec, ospec, ospec, ospec],
        out_shape=[outb, out, out, outb],
        scratch_shapes=[pltpu.VMEM((2 * P_HEADS, tm, LANES), F32),
                        pltpu.VMEM((2, P_NKEYS, tm), F32),
                        pltpu.VMEM((2, P_TOPK, tm), F32),
                        pltpu.VMEM((P_TOPK, tm), F32)],
        compiler_params=_params("parallel", "arbitrary"),
        name="peer_select",
    )(xb, wq, sk)


def _peer_gated_act(act_t, blk, rank2_ref, cnt1_ref, e1_ref, e2_ref, nsub):
    parts = []
    for ii in range(nsub):
        i = blk * nsub + ii
        w = None
        for h in range(P_HEADS):
            cnt = cnt1_ref[h, pl.ds(i, 1), :].astype(BF16)
            e1 = e1_ref[h, pl.ds(i, 1), :].astype(BF16)
            t = jnp.where(rank2_ref[h] < cnt, e2_ref[h], jnp.zeros((), BF16)) * e1
            w = t if w is None else w + t
        a = act_t[ii * P_NKEYS:(ii + 1) * P_NKEYS]
        gelu = 0.5 * a * (1.0 + lax.erf(a * (0.5 ** 0.5)))
        parts.append(w * gelu.astype(BF16))
    return jnp.concatenate(parts, axis=0) if nsub > 1 else parts[0]


def _peer_dense_kernel(xb_ref, rank2_ref, cnt1_ref, e1_ref, e2_ref, u_ref, vta_ref, vtb_ref, o_ref,
                       pa_ref, pb_ref, *, nsub):
    s = pl.program_id(1)
    ns = pl.num_programs(1) - 1
    be = nsub * P_NKEYS
    sel = (rank2_ref, cnt1_ref, e1_ref, e2_ref)

    @pl.when(s == 0)
    def _():
        o_ref[0] = jnp.zeros(o_ref.shape[1:], F32)
        pa_ref[...] = jnp.zeros_like(pa_ref)

    @pl.when(s < ns)
    def _():
        xb = xb_ref[...]
        act0 = _nt_dot(u_ref[0:be, :], xb)
        act1 = _nt_dot(u_ref[be:2 * be, :], xb)
        o_ref[0] += jnp.dot(vta_ref[0], pa_ref[...], preferred_element_type=F32)
        pb_ref[...] = _peer_gated_act(act0, 2 * s, *sel, nsub)
        o_ref[0] += jnp.dot(vtb_ref[0], pb_ref[...], preferred_element_type=F32)
        pa_ref[...] = _peer_gated_act(act1, 2 * s + 1, *sel, nsub)

    @pl.when(s == ns)
    def _():
        o_ref[0] += jnp.dot(vta_ref[0], pa_ref[...], preferred_element_type=F32)


def _peer_dense(xb, rank2, cnt1, e1, e2, u_b, vt_b, tm):
    T, D = xb.shape
    tm = min(tm, T)
    nblk, _, be = vt_b.shape
    nsub = be // P_NKEYS
    ns = nblk // 2
    sel = pl.BlockSpec((P_HEADS, P_NKEYS, tm), lambda i, s: (0, 0, i))
    return pl.pallas_call(
        functools.partial(_peer_dense_kernel, nsub=nsub),
        grid=(T // tm, ns + 1),
        in_specs=[pl.BlockSpec((tm, D), lambda i, s: (i, 0)), sel, sel, sel, sel,
                  pl.BlockSpec((2 * be, D), lambda i, s: (jnp.minimum(s, ns - 1), 0)),
                  pl.BlockSpec((1, D, be), lambda i, s: (jnp.maximum(2 * s - 1, 0), 0, 0)),
                  pl.BlockSpec((1, D, be), lambda i, s: (jnp.minimum(2 * s, nblk - 1), 0, 0))],
        out_specs=pl.BlockSpec((1, D, tm), lambda i, s: (i, 0, 0)),
        out_shape=jax.ShapeDtypeStruct((T // tm, D, tm), F32),
        scratch_shapes=[pltpu.VMEM((be, tm), BF16), pltpu.VMEM((be, tm), BF16)],
        compiler_params=_params("parallel", "arbitrary"),
        name="peer_dense",
    )(xb, rank2, cnt1, e1, e2, u_b, vt_b, vt_b)


def _pad_row(vals_at, width=LANES):
    row = jnp.zeros((1, width), F32)
    for off, v in vals_at:
        row = row.at[0, off:off + v.shape[0]].set(v.astype(F32))
    return row


def _rope_tables(S):
    half = A_HEADDIM // 2
    freqs = ROPE_THETA ** (-jnp.arange(half, dtype=F32) / half)
    ang = jnp.arange(S, dtype=F32)[:, None] * freqs[None, :]
    reps = LANES // half
    return jnp.tile(jnp.cos(ang), (1, reps)), jnp.tile(jnp.sin(ang), (1, reps))


def _head_expand_matrix():
    e = np.zeros((S_GROUPS, LANES, S_HPG * S_HEADDIM), np.float32)
    for g in range(S_GROUPS):
        for hh in range(S_HPG):
            e[g, SM_DT + g * S_HPG + hh, hh * S_HEADDIM:(hh + 1) * S_HEADDIM] = 1.0
    return jnp.asarray(e)


def _layer(x, xb, p, consts, B, S, alpha):
    T, D = x.shape
    w_in = p["w_in"]
    o_m = 2 * M_HEADS * M_DQK + 2 * M_HEADS * M_DV
    o_s = o_m + 2 * M_HEADS
    o_dt = o_s + S_DINNER + S_CONV_CH
    o_a = o_dt + S_HEADS
    o_g = o_a + (A_QHEADS + 2 * A_KVHEADS) * A_HEADDIM
    w_m = w_in[:, :o_m].astype(BF16)
    w_s = w_in[:, o_s:o_dt].astype(BF16)
    w_a = w_in[:, o_a:o_g].astype(BF16)
    w_g = w_in[:, o_g:].astype(BF16)
    w_sm = jnp.concatenate([w_in[:, o_m:o_s], w_in[:, o_dt:o_a],
                            jnp.zeros((D, LANES - 2 * M_HEADS - S_HEADS), F32)], axis=1).astype(BF16)

    mp = _matmul(xb, w_m, 1024, 512)
    sp = _matmul(xb, w_s, 1024, 512)
    ap = _matmul(xb, w_a, 1024, 512)
    gp = _matmul(xb, w_g, 1024, 512)
    sm = _matmul(xb, w_sm, 1024, LANES)

    brow = _pad_row([(SM_I, p["mlstm_gate_b"][0]), (SM_F, p["mlstm_gate_b"][1]),
                     (SM_DT, p["ssm_dt_bias"])])
    arow = _pad_row([(SM_DT, p["ssm_a_log"])])
    drow = jnp.repeat(p["ssm_d"].astype(F32), S_HEADDIM)[None, :]
    y_m = _mlstm(mp, sm, brow, p["mlstm_norm_w"][None, :], B, S)
    y_s = _ssd(sp, sm, brow, arow, drow, p["ssm_conv_w"][:, 0, :], p["ssm_conv_b"][None, :],
               p["ssm_norm_w"][None, :], consts["e_mat"], B, S)
    y_a = _swa(ap, consts["cos"], consts["sin"], _pad_row([(0, p["swa_sinks"])]), B, S)

    mix = _merge(y_m, y_s, y_a, gp, p["merge_gate_b"].reshape(1, 3 * D),
                 p["w_branch"].astype(BF16), 1024, 512)
    x1, x1b = _outproj(x, mix, p["w_out"].astype(BF16), p["ln1_g"][None, :], p["ln1_b"][None, :],
                       alpha, 512)

    rank2, cnt1, e1, e2 = _peer_select(x1b, p["peer_wq"].astype(BF16), p["peer_subkeys"], 512)
    be = 256
    vt_blocks = jnp.swapaxes(p["peer_v"].astype(BF16).reshape(P_EXPERTS // be, be, D), 1, 2)
    peer = _peer_dense(x1b, rank2, cnt1, e1, e2, p["peer_u"].astype(BF16), vt_blocks, 512)
    return _res_ln(x1, peer, p["ln2_g"][None, :], p["ln2_b"][None, :], alpha)


def _forward(x, params, depth):
    B, S, D = x.shape
    alpha = (2.0 * depth) ** 0.25
    cos_t, sin_t = _rope_tables(S)
    consts = {"cos": cos_t, "sin": sin_t, "e_mat": _head_expand_matrix()}
    xf = x.reshape(B * S, D)

    def body(carry, p):
        xc, xcb = carry
        return _layer(xc, xcb, p, consts, B, S, alpha), None

    (xf, _), _ = lax.scan(body, (xf, xf.astype(BF16)), params)
    return xf.reshape(B, S, D)


def kernel(x, w_in, mlstm_gate_b, mlstm_norm_w, ssm_conv_w, ssm_conv_b, ssm_dt_bias, ssm_a_log,
           ssm_d, ssm_norm_w, swa_sinks, merge_gate_b, w_branch, w_out, ln1_g, ln1_b,
           peer_wq, peer_subkeys, peer_u, peer_v, ln2_g, ln2_b):
    params = dict(w_in=w_in, mlstm_gate_b=mlstm_gate_b, mlstm_norm_w=mlstm_norm_w,
                  ssm_conv_w=ssm_conv_w, ssm_conv_b=ssm_conv_b, ssm_dt_bias=ssm_dt_bias,
                  ssm_a_log=ssm_a_log, ssm_d=ssm_d, ssm_norm_w=ssm_norm_w, swa_sinks=swa_sinks,
                  merge_gate_b=merge_gate_b, w_branch=w_branch, w_out=w_out, ln1_g=ln1_g,
                  ln1_b=ln1_b, peer_wq=peer_wq, peer_subkeys=peer_subkeys, peer_u=peer_u,
                  peer_v=peer_v, ln2_g=ln2_g, ln2_b=ln2_b)
    return _forward(x, params, w_in.shape[0])
```

```python
import functools
import math

import jax
import jax.numpy as jnp
import numpy as np
from jax import lax
from jax.experimental import pallas as pl
from jax.experimental.pallas import tpu as pltpu

F32 = jnp.float32
BF16 = jnp.bfloat16
HIGHEST = lax.Precision.HIGHEST

LN_EPS = 1e-5
NORM_EPS = 1e-6

LANES = 128
CHUNK = 128
STEP_CHUNKS = 2
VMEM_LIMIT = 56 * 1024 * 1024

M_HEADS, M_DQK, M_DV = 4, 128, 256
S_HEADS, S_GROUPS, S_HPG, S_HEADDIM, S_DSTATE, S_CONV = 16, 2, 8, 64, 128, 4
S_DINNER = 1024
S_CONV_CH = S_DINNER + 2 * S_GROUPS * S_DSTATE
A_QHEADS, A_KVHEADS, A_HEADDIM, A_REP = 16, 4, 64, 4
ROPE_THETA = 10000.0
P_HEADS, P_NKEYS, P_TOPK = 8, 128, 16
P_EXPERTS = P_NKEYS * P_NKEYS
BRANCH_WIDTH = 1024

SM_I, SM_F, SM_DT = 0, M_HEADS, 2 * M_HEADS

NEG_INF = float("-inf")


def _params(*sem, flags=None):
    return pltpu.CompilerParams(dimension_semantics=sem, vmem_limit_bytes=VMEM_LIMIT, flags=flags)


def _nt_dot(a, b, **kw):
    return lax.dot_general(a, b, (((1,), (1,)), ((), ())), preferred_element_type=F32, **kw)


def _softplus(x):
    return jnp.maximum(x, 0.0) + jnp.log(1.0 + jnp.exp(-jnp.abs(x)))


def _log_sigmoid(x):
    return jnp.minimum(x, 0.0) - jnp.log(1.0 + jnp.exp(-jnp.abs(x)))


def _sigmoid(x):
    return 1.0 / (1.0 + jnp.exp(-x))


def _silu(x):
    return x * _sigmoid(x)


def _layer_norm(y, g, b):
    mu = jnp.mean(y, -1, keepdims=True)
    d = y - mu
    var = jnp.mean(d * d, -1, keepdims=True)
    return d * lax.rsqrt(var + LN_EPS) * g + b


def _mm_kernel(x_ref, w_ref, o_ref):
    o_ref[...] = jnp.dot(x_ref[...], w_ref[...], preferred_element_type=F32).astype(o_ref.dtype)


def _matmul(x, w, tm, tn, out_dtype=F32):
    T, K = x.shape
    N = w.shape[1]
    tm, tn = min(tm, T), min(tn, N)
    return pl.pallas_call(
        _mm_kernel,
        grid=(T // tm, N // tn),
        in_specs=[pl.BlockSpec((tm, K), lambda i, j: (i, 0)),
                  pl.BlockSpec((K, tn), lambda i, j: (0, j))],
        out_specs=pl.BlockSpec((tm, tn), lambda i, j: (i, j)),
        out_shape=jax.ShapeDtypeStruct((T, N), out_dtype),
        compiler_params=_params("parallel", "arbitrary"),
        name="proj_matmul",
    )(x, w)


def _mlstm_kernel(mp_ref, sm_ref, brow_ref, nw_ref, o_ref, c_ref, n_ref, m_ref):
    L = CHUNK

    @pl.when(pl.program_id(1) == 0)
    def _():
        c_ref[...] = jnp.zeros_like(c_ref)
        n_ref[...] = jnp.zeros_like(n_ref)
        m_ref[...] = jnp.zeros_like(m_ref)

    row = lax.broadcasted_iota(jnp.int32, (L, L), 0)
    col = lax.broadcasted_iota(jnp.int32, (L, L), 1)
    tril = row >= col
    trilf = tril.astype(F32)
    scale = M_DQK ** -0.5
    for cc, h in [(cc, h) for cc in range(STEP_CHUNKS) for h in range(M_HEADS)]:
        rows = slice(cc * L, (cc + 1) * L)
        gates = sm_ref[rows, :] + brow_ref[...]
        q = mp_ref[rows, h * M_DQK:(h + 1) * M_DQK]
        k = mp_ref[rows, 512 + h * M_DQK:512 + (h + 1) * M_DQK] * scale
        v = mp_ref[rows, 1024 + h * M_DV:1024 + (h + 1) * M_DV].astype(BF16)
        og = mp_ref[rows, 2048 + h * M_DV:2048 + (h + 1) * M_DV]
        li_col = gates[:, SM_I + h:SM_I + h + 1]
        lf_col = _log_sigmoid(gates[:, SM_F + h:SM_F + h + 1])
        bc = jnp.dot(trilf, jnp.broadcast_to(lf_col, (L, L)), precision=HIGHEST,
                     preferred_element_type=F32)
        br = bc.T
        lir = jnp.broadcast_to(li_col, (L, L)).T
        b_col = bc[:, 0:1]
        m_prev = m_ref[h, 0:1, 0:1]
        log_d = jnp.where(tril, bc - br + lir, NEG_INF)
        m_t = jnp.maximum(jnp.max(log_d, axis=1, keepdims=True), b_col + m_prev)
        w_intra = jnp.exp(log_d - m_t)
        w_inter = jnp.exp(b_col + m_prev - m_t)
        qb = q.astype(BF16)
        s = _nt_dot(qb, k.astype(BF16)) * w_intra
        c_old = c_ref[h]
        n_old = n_ref[h, 0:1, :]
        num = (jnp.dot(s.astype(BF16), v, preferred_element_type=F32)
               + w_inter * jnp.dot(qb, c_old.astype(BF16), preferred_element_type=F32))
        den = (jnp.sum(s, axis=1, keepdims=True)
               + w_inter * jnp.sum(q * n_old, axis=1, keepdims=True))
        hh = num / jnp.maximum(jnp.abs(den), jnp.exp(-m_t))

        g = bc[L - 1:L, 0:1]
        lws = g - b_col + li_col
        m_new = jnp.maximum(g + m_prev, jnp.max(lws, axis=0, keepdims=True))
        a_prev = jnp.exp(g + m_prev - m_new)
        kw = k * jnp.exp(lws - m_new)
        c_ref[h] = a_prev * c_old + jnp.dot(kw.T.astype(BF16), v, preferred_element_type=F32)
        n_ref[h, 0:1, :] = a_prev * n_old + jnp.sum(kw, axis=0, keepdims=True)
        m_ref[h] = jnp.broadcast_to(m_new, m_ref.shape[1:])

        mu = jnp.mean(hh, -1, keepdims=True)
        d = hh - mu
        var = jnp.mean(d * d, -1, keepdims=True)
        hn = d * lax.rsqrt(var + NORM_EPS) * nw_ref[:, h * M_DV:(h + 1) * M_DV]
        o_ref[rows, h * M_DV:(h + 1) * M_DV] = (_sigmoid(og) * hn).astype(o_ref.dtype)


def _mlstm(mp, sm, brow, nw, B, S):
    R = STEP_CHUNKS * CHUNK
    nc = S // R
    W = mp.shape[1]
    return pl.pallas_call(
        _mlstm_kernel,
        grid=(B, nc),
        in_specs=[pl.BlockSpec((R, W), lambda b, c: (b * nc + c, 0)),
                  pl.BlockSpec((R, LANES), lambda b, c: (b * nc + c, 0)),
                  pl.BlockSpec((1, LANES), lambda b, c: (0, 0)),
                  pl.BlockSpec((1, BRANCH_WIDTH), lambda b, c: (0, 0))],
        out_specs=pl.BlockSpec((R, BRANCH_WIDTH), lambda b, c: (b * nc + c, 0)),
        out_shape=jax.ShapeDtypeStruct((B * S, BRANCH_WIDTH), BF16),
        scratch_shapes=[pltpu.VMEM((M_HEADS, M_DQK, M_DV), F32),
                        pltpu.VMEM((M_HEADS, 8, M_DQK), F32),
                        pltpu.VMEM((M_HEADS, 8, LANES), F32)],
        compiler_params=_params("parallel", "arbitrary"),
        name="mlstm",
    )(mp, sm, brow, nw)


def _ssd_kernel(cur_ref, prev_ref, sm_ref, brow_ref, arow_ref, drow_ref, cw_ref, cb_ref, nw_ref,
                e_ref, o_ref, st_ref):
    L = CHUNK
    c = pl.program_id(1)

    @pl.when(c == 0)
    def _():
        st_ref[...] = jnp.zeros_like(st_ref)

    u = cur_ref[:, S_DINNER:]
    up = prev_ref[:, S_DINNER:] * (c > 0).astype(F32)
    rows = lax.broadcasted_iota(jnp.int32, (L, S_CONV_CH), 0)
    conv = cb_ref[...] + cw_ref[S_CONV - 1:S_CONV, :] * u
    for kk in range(S_CONV - 1):
        sh = S_CONV - 1 - kk
        shifted = jnp.where(rows < sh, pltpu.roll(up, sh, 0), pltpu.roll(u, sh, 0))
        conv = conv + cw_ref[kk:kk + 1, :] * shifted
    xbc = _silu(conv)

    dt = _softplus(sm_ref[...] + brow_ref[...])
    d_a = dt * (-jnp.exp(arow_ref[...]))
    row = lax.broadcasted_iota(jnp.int32, (L, L), 0)
    col = lax.broadcasted_iota(jnp.int32, (L, L), 1)
    tril = row >= col
    acum = jnp.dot(tril.astype(F32), d_a, precision=HIGHEST, preferred_element_type=F32)

    gw = S_HPG * S_HEADDIM
    for g in range(S_GROUPS):
        xg = xbc[:, g * gw:(g + 1) * gw]
        bg = xbc[:, S_DINNER + g * S_DSTATE:S_DINNER + (g + 1) * S_DSTATE]
        cg = xbc[:, S_DINNER + (S_GROUPS + g) * S_DSTATE:S_DINNER + (S_GROUPS + g + 1) * S_DSTATE]
        cgb = cg.astype(BF16)
        cb = _nt_dot(cgb, bg.astype(BF16))
        e_g = e_ref[g]
        dt_e = jnp.dot(dt, e_g, precision=HIGHEST, preferred_element_type=F32)
        acum_e = jnp.dot(acum, e_g, precision=HIGHEST, preferred_element_type=F32)
        xc = xg * dt_e
        xcb = xc.astype(BF16)
        last = acum_e[L - 1:L, :]
        ys = []
        for hh in range(S_HPG):
            lane = SM_DT + g * S_HPG + hh
            ac = jnp.broadcast_to(acum[:, lane:lane + 1], (L, L))
            dec = jnp.exp(jnp.where(tril, ac - ac.T, NEG_INF))
            ys.append(jnp.dot((cb * dec).astype(BF16), xcb[:, hh * S_HEADDIM:(hh + 1) * S_HEADDIM],
                              preferred_element_type=F32))
        y = jnp.concatenate(ys, axis=1)
        st = st_ref[g]
        y = y + jnp.dot(cgb, st.astype(BF16), preferred_element_type=F32) * jnp.exp(acum_e)
        upd = jnp.dot(bg.T.astype(BF16), (jnp.exp(last - acum_e) * xc).astype(BF16),
                      preferred_element_type=F32)
        st_ref[g] = jnp.exp(last) * st + upd
        y = y + drow_ref[:, g * gw:(g + 1) * gw] * xg
        y = y * _silu(cur_ref[:, g * gw:(g + 1) * gw])
        y = y * lax.rsqrt(jnp.mean(y * y, -1, keepdims=True) + NORM_EPS)
        o_ref[:, g * gw:(g + 1) * gw] = (y * nw_ref[:, g * gw:(g + 1) * gw]).astype(o_ref.dtype)


def _ssd(sp, sm, brow, arow, drow, cw, cb, nw, e_mat, B, S):
    nc = S // CHUNK
    W = sp.shape[1]
    const = lambda b, c: (0, 0)
    return pl.pallas_call(
        _ssd_kernel,
        grid=(B, nc),
        in_specs=[pl.BlockSpec((CHUNK, W), lambda b, c: (b * nc + c, 0)),
                  pl.BlockSpec((CHUNK, W), lambda b, c: (b * nc + jnp.maximum(c - 1, 0), 0)),
                  pl.BlockSpec((CHUNK, LANES), lambda b, c: (b * nc + c, 0)),
                  pl.BlockSpec((1, LANES), const),
                  pl.BlockSpec((1, LANES), const),
                  pl.BlockSpec((1, S_DINNER), const),
                  pl.BlockSpec((S_CONV, S_CONV_CH), const),
                  pl.BlockSpec((1, S_CONV_CH), const),
                  pl.BlockSpec((1, S_DINNER), const),
                  pl.BlockSpec((S_GROUPS, LANES, S_HPG * S_HEADDIM), lambda b, c: (0, 0, 0))],
        out_specs=pl.BlockSpec((CHUNK, S_DINNER), lambda b, c: (b * nc + c, 0)),
        out_shape=jax.ShapeDtypeStruct((B * S, S_DINNER), BF16),
        scratch_shapes=[pltpu.VMEM((S_GROUPS, S_DSTATE, S_HPG * S_HEADDIM), F32)],
        compiler_params=_params("parallel", "arbitrary"),
        name="ssd",
    )(sp, sp, sm, brow, arow, drow, cw, cb, nw, e_mat)


def _rope(x, cos, sin):
    lane = lax.broadcasted_iota(jnp.int32, x.shape, 1)
    first_half = (lane % A_HEADDIM) < (A_HEADDIM // 2)
    rot = jnp.where(first_half, -pltpu.roll(x, LANES - A_HEADDIM // 2, 1),
                    pltpu.roll(x, A_HEADDIM // 2, 1))
    return x * cos + rot * sin


def _swa_kernel(cur_ref, prev_ref, cos_ref, sin_ref, cosp_ref, sinp_ref, sink_ref, o_ref):
    L = CHUNK
    n = pl.program_id(1)
    cos, sin = cos_ref[...], sin_ref[...]
    cosp, sinp = cosp_ref[...], sinp_ref[...]
    qo, ko, vo = 0, A_QHEADS * A_HEADDIM, (A_QHEADS + A_KVHEADS) * A_HEADDIM
    kc = jnp.concatenate([_rope(cur_ref[:, ko + j * LANES:ko + (j + 1) * LANES], cos, sin)
                          for j in range(2)], axis=1).astype(BF16)
    kp = jnp.concatenate([_rope(prev_ref[:, ko + j * LANES:ko + (j + 1) * LANES], cosp, sinp)
                          for j in range(2)], axis=1).astype(BF16)
    kk = jnp.concatenate([kp, kc], axis=0)
    vw = A_KVHEADS * A_HEADDIM
    vv_t = jnp.concatenate([prev_ref[:, vo:vo + vw], cur_ref[:, vo:vo + vw]], axis=0).T
    krow = lax.broadcasted_iota(jnp.int32, (2 * L, L), 0)
    qcol = lax.broadcasted_iota(jnp.int32, (2 * L, L), 1)
    first_key = jnp.where(n > 0, 0, L)
    valid = jnp.logical_and(jnp.logical_and(krow > qcol, krow <= qcol + L), krow >= first_key)
    scale = A_HEADDIM ** -0.5
    for j in range(A_QHEADS // 2):
        qpair = _rope(cur_ref[:, qo + j * LANES:qo + (j + 1) * LANES], cos, sin).astype(BF16)
        outs = []
        for t in range(2):
            hq = 2 * j + t
            g = hq // A_REP
            sl = slice(g * A_HEADDIM, (g + 1) * A_HEADDIM)
            qh = qpair[:, t * A_HEADDIM:(t + 1) * A_HEADDIM]
            s_t = jnp.where(valid, _nt_dot(kk[:, sl], qh) * scale, NEG_INF)
            sink = sink_ref[:, hq:hq + 1]
            m = jnp.maximum(jnp.max(s_t, axis=0, keepdims=True), sink)
            p = jnp.exp(s_t - m)
            den = jnp.sum(p, axis=0, keepdims=True) + jnp.exp(sink - m)
            o_t = jnp.dot(vv_t[sl, :].astype(BF16), p.astype(BF16),
                          preferred_element_type=F32)
            outs.append(o_t / den)
        o_ref[:, j * LANES:(j + 1) * LANES] = jnp.concatenate(outs, axis=0).T.astype(o_ref.dtype)


def _swa(ap, cos_t, sin_t, sink_row, B, S):
    nb = S // CHUNK
    W = ap.shape[1]
    cur = lambda b, n: (b * nb + n, 0)
    prev = lambda b, n: (b * nb + jnp.maximum(n - 1, 0), 0)
    tab = lambda b, n: (n, 0)
    tabp = lambda b, n: (jnp.maximum(n - 1, 0), 0)
    return pl.pallas_call(
        _swa_kernel,
        grid=(B, nb),
        in_specs=[pl.BlockSpec((CHUNK, W), cur), pl.BlockSpec((CHUNK, W), prev),
                  pl.BlockSpec((CHUNK, LANES), tab), pl.BlockSpec((CHUNK, LANES), tab),
                  pl.BlockSpec((CHUNK, LANES), tabp), pl.BlockSpec((CHUNK, LANES), tabp),
                  pl.BlockSpec((1, LANES), lambda b, n: (0, 0))],
        out_specs=pl.BlockSpec((CHUNK, BRANCH_WIDTH), cur),
        out_shape=jax.ShapeDtypeStruct((B * S, BRANCH_WIDTH), BF16),
        compiler_params=_params("parallel", "parallel"),
        name="swa",
    )(ap, ap, cos_t, sin_t, cos_t, sin_t, sink_row)


def _merge_kernel(ym_ref, ys_ref, ya_ref, g0_ref, g1_ref, g2_ref, b0_ref, b1_ref, b2_ref,
                  w0_ref, w1_ref, w2_ref, o_ref):
    acc = None
    for y_ref, g_ref, b_ref, w_ref in ((ym_ref, g0_ref, b0_ref, w0_ref),
                                       (ys_ref, g1_ref, b1_ref, w1_ref),
                                       (ya_ref, g2_ref, b2_ref, w2_ref)):
        t = _sigmoid(g_ref[...] + b_ref[...]) * jnp.dot(y_ref[...], w_ref[0],
                                                        preferred_element_type=F32)
        acc = t if acc is None else acc + t
    o_ref[...] = acc.astype(o_ref.dtype)


def _merge(ym, ys, ya, gp, gb, wb, tm, tn):
    T = ym.shape[0]
    D = wb.shape[2]
    tm, tn = min(tm, T), min(tn, D)
    nj = D // tn
    yspec = pl.BlockSpec((tm, BRANCH_WIDTH), lambda i, j: (i, 0))
    gspec = lambda k: pl.BlockSpec((tm, tn), lambda i, j: (i, k * nj + j))
    bspec = lambda k: pl.BlockSpec((1, tn), lambda i, j: (0, k * nj + j))
    wspec = lambda k: pl.BlockSpec((1, BRANCH_WIDTH, tn), lambda i, j: (k, 0, j))
    return pl.pallas_call(
        _merge_kernel,
        grid=(T // tm, nj),
        in_specs=[yspec, yspec, yspec, gspec(0), gspec(1), gspec(2), bspec(0), bspec(1), bspec(2),
                  wspec(0), wspec(1), wspec(2)],
        out_specs=pl.BlockSpec((tm, tn), lambda i, j: (i, j)),
        out_shape=jax.ShapeDtypeStruct((T, D), BF16),
        compiler_params=_params("parallel", "arbitrary"),
        name="merge",
    )(ym, ys, ya, gp, gp, gp, gb, gb, gb, wb, wb, wb)


def _outproj_kernel(x_ref, mix_ref, w_ref, g_ref, b_ref, o_ref, ob_ref, *, alpha):
    y = alpha * x_ref[...] + jnp.dot(mix_ref[...], w_ref[...], preferred_element_type=F32)
    r = _layer_norm(y, g_ref[...], b_ref[...])
    o_ref[...] = r
    ob_ref[...] = r.astype(BF16)


def _outproj(x, mix, w, g, b, alpha, tm):
    T, D = x.shape
    tm = min(tm, T)
    rowspec = pl.BlockSpec((tm, D), lambda i: (i, 0))
    vec = pl.BlockSpec((1, D), lambda i: (0, 0))
    return pl.pallas_call(
        functools.partial(_outproj_kernel, alpha=alpha),
        grid=(T // tm,),
        in_specs=[rowspec, rowspec, pl.BlockSpec((D, D), lambda i: (0, 0)), vec, vec],
        out_specs=[rowspec, rowspec],
        out_shape=[jax.ShapeDtypeStruct((T, D), F32), jax.ShapeDtypeStruct((T, D), BF16)],
        compiler_params=_params("parallel"),
        name="outproj_ln",
    )(x, mix, w, g, b)


def _res_ln_kernel(x_ref, yt_ref, g_ref, b_ref, o_ref, ob_ref, *, alpha):
    r = _layer_norm(alpha * x_ref[...] + yt_ref[0].T, g_ref[...], b_ref[...])
    o_ref[...] = r
    ob_ref[...] = r.astype(BF16)


def _res_ln(x, yt, g, b, alpha):
    T, D = x.shape
    tm = yt.shape[2]
    rowspec = pl.BlockSpec((tm, D), lambda i: (i, 0))
    vec = pl.BlockSpec((1, D), lambda i: (0, 0))
    return pl.pallas_call(
        functools.partial(_res_ln_kernel, alpha=alpha),
        grid=(T // tm,),
        in_specs=[rowspec, pl.BlockSpec((1, D, tm), lambda i: (i, 0, 0)), vec, vec],
        out_specs=[rowspec, rowspec],
        out_shape=[jax.ShapeDtypeStruct((T, D), F32), jax.ShapeDtypeStruct((T, D), BF16)],
        compiler_params=_params("parallel"),
        name="res_ln",
    )(x, yt, g, b)


_CAND_NQ = (16, 8, 5, 4, 3, 2, 2, 2)
_CAND_ROWS = 16 + 7 * 8 + 8


def _extract_top(work, iota, n_iter, on_pick):
    big = float(work.shape[0])
    for p in range(n_iter):
        m = jnp.max(work, axis=0, keepdims=True)
        idx = jnp.min(jnp.where(work == m, iota, big), axis=0, keepdims=True)
        onehot = iota == idx
        work = jnp.where(onehot, NEG_INF, work)
        on_pick(p, m, onehot)


def _select_chunk(s1, s2, top_ref, cnt_ref, lanes, exact):
    K = P_TOPK
    n = s1.shape[1]
    iota = lax.broadcasted_iota(jnp.int32, (P_NKEYS, n), 0).astype(F32)
    ranks = []
    suspect = jnp.zeros((1, n), F32)
    for c, s in enumerate((s1, s2)):
        rank = jnp.full((P_NKEYS, n), float(K), F32)
        if exact:
            rank_holder = [rank]

            def on_pick(p, m, onehot, c=c, rank_holder=rank_holder):
                top_ref[c, p:p + 1, lanes] = m
                rank_holder[0] = jnp.where(onehot, float(p), rank_holder[0])

            _extract_top(s, iota, K, on_pick)
            rank = rank_holder[0]
        else:
            work = s
            for p in range(K):
                m = jnp.max(work, axis=0, keepdims=True)
                hit = work == m
                work = jnp.where(hit, NEG_INF, work)
                rank = jnp.where(hit, float(p), rank)
                top_ref[c, p:p + 1, lanes] = m
            n_ranked = jnp.sum(jnp.where(rank < float(K), 1.0, 0.0), axis=0, keepdims=True)
            suspect = suspect + jnp.abs(n_ranked - float(K))
        ranks.append(rank)

    a = top_ref[0, :, lanes]
    b = top_ref[1, :, lanes]
    ea = jnp.exp(a - a[0:1])
    eb = jnp.exp(b - b[0:1])
    q8 = lax.broadcasted_iota(jnp.int32, (8, n), 0)
    cand = [a[0:1] + b]
    wcand = [ea[0:1] * eb]
    for p in range(1, 8):
        ok = q8 < _CAND_NQ[p]
        cand.append(jnp.where(ok, a[p:p + 1] + b[0:8], NEG_INF))
        wcand.append(ea[p:p + 1] * eb[0:8])
    cand.append(a[8:16] + b[0:1])
    wcand.append(ea[8:16] * eb[0:1])
    cand = jnp.concatenate(cand, axis=0)
    wcand = jnp.concatenate(wcand, axis=0)
    if exact:
        iota_c = lax.broadcasted_iota(jnp.int32, (_CAND_ROWS, n), 0).astype(F32)
        sel_holder = [jnp.zeros((_CAND_ROWS, n), F32)]

        def on_pick2(p, m, onehot):
            sel_holder[0] = jnp.where(onehot, 1.0, sel_holder[0])

        _extract_top(cand, iota_c, K, on_pick2)
        sel = sel_holder[0]
    else:
        work = cand
        for p in range(K):
            m = jnp.max(work, axis=0, keepdims=True)
            work = jnp.where(work == m, NEG_INF, work)
        sel = jnp.where(work != cand, 1.0, 0.0)
        suspect = suspect + jnp.abs(jnp.sum(sel, axis=0, keepdims=True) - float(K))
    z = jnp.sum(sel * wcand, axis=0, keepdims=True)
    cnt_ref[0:1, lanes] = jnp.sum(sel[0:16], axis=0, keepdims=True)
    for p in range(1, 8):
        cnt_ref[p:p + 1, lanes] = jnp.sum(sel[8 + 8 * p:16 + 8 * p], axis=0, keepdims=True)
    cnt_ref[8:16, lanes] = sel[72:80]
    cnt = cnt_ref[:, lanes]
    cnt1 = jnp.zeros((P_NKEYS, n), F32)
    for p in range(K):
        cnt1 = jnp.where(ranks[0] == float(p), cnt[p:p + 1], cnt1)
    return ranks[1], cnt1, jnp.exp(s1 - a[0:1]), jnp.exp(s2 - b[0:1]) / z, suspect


def _peer_select_kernel(xb_ref, wq_ref, sk_ref, rank2_ref, cnt1_ref, e1_ref, e2_ref,
                        q_scr, s_scr, top_scr, cnt_scr):
    h = pl.program_id(1)
    tm = xb_ref.shape[0]

    @pl.when(h == 0)
    def _():
        q = jnp.dot(xb_ref[...], wq_ref[...], preferred_element_type=F32)
        for hc in range(2 * P_HEADS):
            q_scr[hc] = q[:, hc * LANES:(hc + 1) * LANES]

    for c in range(2):
        s_scr[c] = _nt_dot(sk_ref[0, c], q_scr[2 * h + c], precision=HIGHEST)

    def select_tile(exact):
        suspect = jnp.zeros((1, LANES), F32)
        for lc in range(tm // LANES):
            lanes = slice(lc * LANES, (lc + 1) * LANES)
            rank2, cnt1, e1, e2, sus = _select_chunk(s_scr[0, :, lanes], s_scr[1, :, lanes],
                                                     top_scr, cnt_scr, lanes, exact)
            rank2_ref[0, :, lanes] = rank2.astype(rank2_ref.dtype)
            cnt1_ref[0, :, lanes] = cnt1
            e1_ref[0, :, lanes] = e1
            e2_ref[0, :, lanes] = e2.astype(e2_ref.dtype)
            suspect = suspect + sus
        return suspect

    suspect = select_tile(exact=False)

    @pl.when(jnp.max(suspect) > 0.0)
    def _():
        select_tile(exact=True)


def _peer_select(xb, wq, sk, tm):
    T, D = xb.shape
    tm = min(tm, T)
    out = jax.ShapeDtypeStruct((P_HEADS, P_NKEYS, T), F32)
    outb = jax.ShapeDtypeStruct((P_HEADS, P_NKEYS, T), BF16)
    ospec = pl.BlockSpec((1, P_NKEYS, tm), lambda i, h: (h, 0, i))
    return pl.pallas_call(
        _peer_select_kernel,
        grid=(T // tm, P_HEADS),
        in_specs=[pl.BlockSpec((tm, D), lambda i, h: (i, 0)),
                  pl.BlockSpec(wq.shape, lambda i, h: (0, 0)),
                  pl.BlockSpec((1, 2, P_NKEYS, LANES), lambda i, h: (h, 0, 0, 0))],
        out_specs=[ospec, ospec, ospec, ospec],
        out_shape=[outb, out, out, outb],
        scratch_shapes=[pltpu.VMEM((2 * P_HEADS, tm, LANES), F32),
                        pltpu.VMEM((2, P_NKEYS, tm), F32),
                        pltpu.VMEM((2, P_TOPK, tm), F32),
                        pltpu.VMEM((P_TOPK, tm), F32)],
        compiler_params=_params("parallel", "arbitrary"),
        name="peer_select",
    )(xb, wq, sk)


def _peer_gated_act(act_t, blk, rank2_ref, cnt1_ref, e1_ref, e2_ref, nsub):
    parts = []
    for ii in range(nsub):
        i = blk * nsub + ii
        w = None
        for h in range(P_HEADS):
            cnt = cnt1_ref[h, pl.ds(i, 1), :].astype(BF16)
            e1 = e1_ref[h, pl.ds(i, 1), :].astype(BF16)
            t = jnp.where(rank2_ref[h] < cnt, e2_ref[h], jnp.zeros((), BF16)) * e1
            w = t if w is None else w + t
        a = act_t[ii * P_NKEYS:(ii + 1) * P_NKEYS]
        gelu = 0.5 * a * (1.0 + lax.erf(a * (0.5 ** 0.5)))
        parts.append(w * gelu.astype(BF16))
    return jnp.concatenate(parts, axis=0) if nsub > 1 else parts[0]


def _peer_dense_kernel(xb_ref, rank2_ref, cnt1_ref, e1_ref, e2_ref, u_ref, vta_ref, vtb_ref, o_ref,
                       pa_ref, pb_ref, *, nsub):
    s = pl.program_id(1)
    ns = pl.num_programs(1) - 1
    sub = nsub * P_NKEYS
    sel = (rank2_ref, cnt1_ref, e1_ref, e2_ref)

    @pl.when(s == 0)
    def _():
        o_ref[0] = jnp.zeros(o_ref.shape[1:], F32)
        pa_ref[...] = jnp.zeros_like(pa_ref)

    @pl.when(s < ns)
    def _():
        xb = xb_ref[...]
        act0 = _nt_dot(u_ref[0:sub, :], xb)
        act1 = _nt_dot(u_ref[sub:2 * sub, :], xb)
        o_ref[0] += jnp.dot(vta_ref[0], pa_ref[...], preferred_element_type=F32)
        pb_ref[...] = _peer_gated_act(act0, 2 * s, *sel, nsub)
        o_ref[0] += jnp.dot(vtb_ref[0], pb_ref[...], preferred_element_type=F32)
        pa_ref[...] = _peer_gated_act(act1, 2 * s + 1, *sel, nsub)

    @pl.when(s == ns)
    def _():
        o_ref[0] += jnp.dot(vta_ref[0], pa_ref[...], preferred_element_type=F32)


def _peer_dense(xb, rank2, cnt1, e1, e2, u_b, vt_b, tm):
    T, D = xb.shape
    tm = min(tm, T)
    nblk, _, be = vt_b.shape
    nsub = be // P_NKEYS
    ns = nblk // 2
    sel = pl.BlockSpec((P_HEADS, P_NKEYS, tm), lambda i, s: (0, 0, i))
    return pl.pallas_call(
        functools.partial(_peer_dense_kernel, nsub=nsub),
        grid=(T // tm, ns + 1),
        in_specs=[pl.BlockSpec((tm, D), lambda i, s: (i, 0)), sel, sel, sel, sel,
                  pl.BlockSpec((2 * be, D), lambda i, s: (jnp.minimum(s, ns - 1), 0)),
                  pl.BlockSpec((1, D, be), lambda i, s: (jnp.maximum(2 * s - 1, 0), 0, 0)),
                  pl.BlockSpec((1, D, be), lambda i, s: (jnp.minimum(2 * s, nblk - 1), 0, 0))],
        out_specs=pl.BlockSpec((1, D, tm), lambda i, s: (i, 0, 0)),
        out_shape=jax.ShapeDtypeStruct((T // tm, D, tm), F32),
        scratch_shapes=[pltpu.VMEM((be, tm), BF16), pltpu.VMEM((be, tm), BF16)],
        compiler_params=_params("parallel", "arbitrary"),
        name="peer_dense",
    )(xb, rank2, cnt1, e1, e2, u_b, vt_b, vt_b)


def _pad_row(vals_at, width=LANES):
    row = jnp.zeros((1, width), F32)
    for off, v in vals_at:
        row = row.at[0, off:off + v.shape[0]].set(v.astype(F32))
    return row


def _rope_tables(S):
    half = A_HEADDIM // 2
    freqs = ROPE_THETA ** (-jnp.arange(half, dtype=F32) / half)
    ang = jnp.arange(S, dtype=F32)[:, None] * freqs[None, :]
    reps = LANES // half
    return jnp.tile(jnp.cos(ang), (1, reps)), jnp.tile(jnp.sin(ang), (1, reps))


def _head_expand_matrix():
    e = np.zeros((S_GROUPS, LANES, S_HPG * S_HEADDIM), np.float32)
    for g in range(S_GROUPS):
        for hh in range(S_HPG):
            e[g, SM_DT + g * S_HPG + hh, hh * S_HEADDIM:(hh + 1) * S_HEADDIM] = 1.0
    return jnp.asarray(e)


def _layer(x, xb, p, consts, B, S, alpha):
    T, D = x.shape
    w_in = p["w_in"]
    o_m = 2 * M_HEADS * M_DQK + 2 * M_HEADS * M_DV
    o_s = o_m + 2 * M_HEADS
    o_dt = o_s + S_DINNER + S_CONV_CH
    o_a = o_dt + S_HEADS
    o_g = o_a + (A_QHEADS + 2 * A_KVHEADS) * A_HEADDIM
    w_m = w_in[:, :o_m].astype(BF16)
    w_s = w_in[:, o_s:o_dt].astype(BF16)
    w_a = w_in[:, o_a:o_g].astype(BF16)
    w_g = w_in[:, o_g:].astype(BF16)
    w_sm = jnp.concatenate([w_in[:, o_m:o_s], w_in[:, o_dt:o_a],
                            jnp.zeros((D, LANES - 2 * M_HEADS - S_HEADS), F32)], axis=1).astype(BF16)

    mp = _matmul(xb, w_m, 1024, 1536)
    sp = _matmul(xb, w_s, 1024, 1280)
    ap = _matmul(xb, w_a, 1024, 1536)
    gp = _matmul(xb, w_g, 1024, 2048)
    sm = _matmul(xb, w_sm, 1024, LANES)

    brow = _pad_row([(SM_I, p["mlstm_gate_b"][0]), (SM_F, p["mlstm_gate_b"][1]),
                     (SM_DT, p["ssm_dt_bias"])])
    arow = _pad_row([(SM_DT, p["ssm_a_log"])])
    drow = jnp.repeat(p["ssm_d"].astype(F32), S_HEADDIM)[None, :]
    y_m = _mlstm(mp, sm, brow, p["mlstm_norm_w"][None, :], B, S)
    y_s = _ssd(sp, sm, brow, arow, drow, p["ssm_conv_w"][:, 0, :], p["ssm_conv_b"][None, :],
               p["ssm_norm_w"][None, :], consts["e_mat"], B, S)
    y_a = _swa(ap, consts["cos"], consts["sin"], _pad_row([(0, p["swa_sinks"])]), B, S)

    mix = _merge(y_m, y_s, y_a, gp, p["merge_gate_b"].reshape(1, 3 * D),
                 p["w_branch"].astype(BF16), 1024, 512)
    x1, x1b = _outproj(x, mix, p["w_out"].astype(BF16), p["ln1_g"][None, :], p["ln1_b"][None, :],
                       alpha, 512)

    rank2, cnt1, e1, e2 = _peer_select(x1b, p["peer_wq"].astype(BF16), p["peer_subkeys"], 512)
    be = 256
    vt_blocks = jnp.swapaxes(p["peer_v"].astype(BF16).reshape(P_EXPERTS // be, be, D), 1, 2)
    peer = _peer_dense(x1b, rank2, cnt1, e1, e2, p["peer_u"].astype(BF16), vt_blocks, 512)
    return _res_ln(x1, peer, p["ln2_g"][None, :], p["ln2_b"][None, :], alpha)


def _forward(x, params, depth):
    B, S, D = x.shape
    alpha = (2.0 * depth) ** 0.25
    cos_t, sin_t = _rope_tables(S)
    consts = {"cos": cos_t, "sin": sin_t, "e_mat": _head_expand_matrix()}
    xf = x.reshape(B * S, D)

    def body(carry, p):
        xc, xcb = carry
        return _layer(xc, xcb, p, consts, B, S, alpha), None

    (xf, _), _ = lax.scan(body, (xf, xf.astype(BF16)), params)
    return xf.reshape(B, S, D)


def kernel(x, w_in, mlstm_gate_b, mlstm_norm_w, ssm_conv_w, ssm_conv_b, ssm_dt_bias, ssm_a_log,
           ssm_d, ssm_norm_w, swa_sinks, merge_gate_b, w_branch, w_out, ln1_g, ln1_b,
           peer_wq, peer_subkeys, peer_u, peer_v, ln2_g, ln2_b):
    params = dict(w_in=w_in, mlstm_gate_b=mlstm_gate_b, mlstm_norm_w=mlstm_norm_w,
                  ssm_conv_w=ssm_conv_w, ssm_conv_b=ssm_conv_b, ssm_dt_bias=ssm_dt_bias,
                  ssm_a_log=ssm_a_log, ssm_d=ssm_d, ssm_norm_w=ssm_norm_w, swa_sinks=swa_sinks,
                  merge_gate_b=merge_gate_b, w_branch=w_branch, w_out=w_out, ln1_g=ln1_g,
                  ln1_b=ln1_b, peer_wq=peer_wq, peer_subkeys=peer_subkeys, peer_u=peer_u,
                  peer_v=peer_v, ln2_g=ln2_g, ln2_b=ln2_b)
    return _forward(x, params, w_in.shape[0])
```

```python
import functools
import math

import jax
import jax.numpy as jnp
import numpy as np
from jax import lax
from jax.experimental import pallas as pl
from jax.experimental.pallas import tpu as pltpu

F32 = jnp.float32
BF16 = jnp.bfloat16
HIGHEST = lax.Precision.HIGHEST

LN_EPS = 1e-5
NORM_EPS = 1e-6

LANES = 128
CHUNK = 128
STEP_CHUNKS = 4
VMEM_LIMIT = 56 * 1024 * 1024

M_HEADS, M_DQK, M_DV = 4, 128, 256
S_HEADS, S_GROUPS, S_HPG, S_HEADDIM, S_DSTATE, S_CONV = 16, 2, 8, 64, 128, 4
S_DINNER = 1024
S_CONV_CH = S_DINNER + 2 * S_GROUPS * S_DSTATE
A_QHEADS, A_KVHEADS, A_HEADDIM, A_REP = 16, 4, 64, 4
ROPE_THETA = 10000.0
P_HEADS, P_NKEYS, P_TOPK = 8, 128, 16
P_EXPERTS = P_NKEYS * P_NKEYS
BRANCH_WIDTH = 1024

SM_I, SM_F, SM_DT = 0, M_HEADS, 2 * M_HEADS

NEG_INF = float("-inf")


def _params(*sem, flags=None):
    return pltpu.CompilerParams(dimension_semantics=sem, vmem_limit_bytes=VMEM_LIMIT, flags=flags)


def _nt_dot(a, b, **kw):
    return lax.dot_general(a, b, (((1,), (1,)), ((), ())), preferred_element_type=F32, **kw)


def _softplus(x):
    return jnp.maximum(x, 0.0) + jnp.log(1.0 + jnp.exp(-jnp.abs(x)))


def _log_sigmoid(x):
    return jnp.minimum(x, 0.0) - jnp.log(1.0 + jnp.exp(-jnp.abs(x)))


def _sigmoid(x):
    return 1.0 / (1.0 + jnp.exp(-x))


def _silu(x):
    return x * _sigmoid(x)


def _layer_norm(y, g, b):
    mu = jnp.mean(y, -1, keepdims=True)
    d = y - mu
    var = jnp.mean(d * d, -1, keepdims=True)
    return d * lax.rsqrt(var + LN_EPS) * g + b


def _mm_kernel(x_ref, w_ref, o_ref):
    o_ref[...] = jnp.dot(x_ref[...], w_ref[...], preferred_element_type=F32).astype(o_ref.dtype)


def _matmul(x, w, tm, tn, out_dtype=F32):
    T, K = x.shape
    N = w.shape[1]
    tm, tn = min(tm, T), min(tn, N)
    return pl.pallas_call(
        _mm_kernel,
        grid=(T // tm, N // tn),
        in_specs=[pl.BlockSpec((tm, K), lambda i, j: (i, 0)),
                  pl.BlockSpec((K, tn), lambda i, j: (0, j))],
        out_specs=pl.BlockSpec((tm, tn), lambda i, j: (i, j)),
        out_shape=jax.ShapeDtypeStruct((T, N), out_dtype),
        compiler_params=_params("parallel", "arbitrary"),
        name="proj_matmul",
    )(x, w)


def _mlstm_kernel(mp_ref, sm_ref, brow_ref, nw_ref, o_ref, c_ref, n_ref, m_ref):
    L = CHUNK

    @pl.when(pl.program_id(1) == 0)
    def _():
        c_ref[...] = jnp.zeros_like(c_ref)
        n_ref[...] = jnp.zeros_like(n_ref)
        m_ref[...] = jnp.zeros_like(m_ref)

    row = lax.broadcasted_iota(jnp.int32, (L, L), 0)
    col = lax.broadcasted_iota(jnp.int32, (L, L), 1)
    tril = row >= col
    trilf = tril.astype(F32)
    scale = M_DQK ** -0.5
    for cc, h in [(cc, h) for cc in range(STEP_CHUNKS) for h in range(M_HEADS)]:
        rows = slice(cc * L, (cc + 1) * L)
        gates = sm_ref[rows, :] + brow_ref[...]
        q = mp_ref[rows, h * M_DQK:(h + 1) * M_DQK]
        k = mp_ref[rows, 512 + h * M_DQK:512 + (h + 1) * M_DQK] * scale
        v = mp_ref[rows, 1024 + h * M_DV:1024 + (h + 1) * M_DV].astype(BF16)
        og = mp_ref[rows, 2048 + h * M_DV:2048 + (h + 1) * M_DV]
        li_col = gates[:, SM_I + h:SM_I + h + 1]
        lf_col = _log_sigmoid(gates[:, SM_F + h:SM_F + h + 1])
        bc = jnp.dot(trilf, jnp.broadcast_to(lf_col, (L, L)), precision=HIGHEST,
                     preferred_element_type=F32)
        br = bc.T
        lir = jnp.broadcast_to(li_col, (L, L)).T
        b_col = bc[:, 0:1]
        m_prev = m_ref[h, 0:1, 0:1]
        log_d = jnp.where(tril, bc - br + lir, NEG_INF)
        m_t = jnp.maximum(jnp.max(log_d, axis=1, keepdims=True), b_col + m_prev)
        w_intra = jnp.exp(log_d - m_t)
        w_inter = jnp.exp(b_col + m_prev - m_t)
        qb = q.astype(BF16)
        s = _nt_dot(qb, k.astype(BF16)) * w_intra
        c_old = c_ref[h]
        n_old = n_ref[h, 0:1, :]
        num = (jnp.dot(s.astype(BF16), v, preferred_element_type=F32)
               + w_inter * jnp.dot(qb, c_old.astype(BF16), preferred_element_type=F32))
        den = (jnp.sum(s, axis=1, keepdims=True)
               + w_inter * jnp.sum(q * n_old, axis=1, keepdims=True))
        hh = num / jnp.maximum(jnp.abs(den), jnp.exp(-m_t))

        g = bc[L - 1:L, 0:1]
        lws = g - b_col + li_col
        m_new = jnp.maximum(g + m_prev, jnp.max(lws, axis=0, keepdims=True))
        a_prev = jnp.exp(g + m_prev - m_new)
        kw = k * jnp.exp(lws - m_new)
        c_ref[h] = a_prev * c_old + jnp.dot(kw.T.astype(BF16), v, preferred_element_type=F32)
        n_ref[h, 0:1, :] = a_prev * n_old + jnp.sum(kw, axis=0, keepdims=True)
        m_ref[h] = jnp.broadcast_to(m_new, m_ref.shape[1:])

        mu = jnp.mean(hh, -1, keepdims=True)
        d = hh - mu
        var = jnp.mean(d * d, -1, keepdims=True)
        hn = d * lax.rsqrt(var + NORM_EPS) * nw_ref[:, h * M_DV:(h + 1) * M_DV]
        o_ref[rows, h * M_DV:(h + 1) * M_DV] = (_sigmoid(og) * hn).astype(o_ref.dtype)


def _mlstm(mp, sm, brow, nw, B, S):
    R = STEP_CHUNKS * CHUNK
    nc = S // R
    W = mp.shape[1]
    return pl.pallas_call(
        _mlstm_kernel,
        grid=(B, nc),
        in_specs=[pl.BlockSpec((R, W), lambda b, c: (b * nc + c, 0)),
                  pl.BlockSpec((R, LANES), lambda b, c: (b * nc + c, 0)),
                  pl.BlockSpec((1, LANES), lambda b, c: (0, 0)),
                  pl.BlockSpec((1, BRANCH_WIDTH), lambda b, c: (0, 0))],
        out_specs=pl.BlockSpec((R, BRANCH_WIDTH), lambda b, c: (b * nc + c, 0)),
        out_shape=jax.ShapeDtypeStruct((B * S, BRANCH_WIDTH), BF16),
        scratch_shapes=[pltpu.VMEM((M_HEADS, M_DQK, M_DV), F32),
                        pltpu.VMEM((M_HEADS, 8, M_DQK), F32),
                        pltpu.VMEM((M_HEADS, 8, LANES), F32)],
        compiler_params=_params("parallel", "arbitrary"),
        name="mlstm",
    )(mp, sm, brow, nw)


def _ssd_kernel(cur_ref, prev_ref, sm_ref, brow_ref, arow_ref, drow_ref, cw_ref, cb_ref, nw_ref,
                e_ref, o_ref, st_ref):
    L = CHUNK
    c = pl.program_id(1)

    @pl.when(c == 0)
    def _():
        st_ref[...] = jnp.zeros_like(st_ref)

    ridx = lax.broadcasted_iota(jnp.int32, (L, S_CONV_CH), 0)
    row = lax.broadcasted_iota(jnp.int32, (L, L), 0)
    col = lax.broadcasted_iota(jnp.int32, (L, L), 1)
    tril = row >= col
    gw = S_HPG * S_HEADDIM
    for cc in range(STEP_CHUNKS):
        rows = slice(cc * L, (cc + 1) * L)
        u = cur_ref[rows, S_DINNER:]
        if cc == 0:
            up = prev_ref[:, S_DINNER:] * (c > 0).astype(F32)
        else:
            up = cur_ref[(cc - 1) * L:cc * L, S_DINNER:]
        conv = cb_ref[...] + cw_ref[S_CONV - 1:S_CONV, :] * u
        for kk in range(S_CONV - 1):
            sh = S_CONV - 1 - kk
            shifted = jnp.where(ridx < sh, pltpu.roll(up, sh, 0), pltpu.roll(u, sh, 0))
            conv = conv + cw_ref[kk:kk + 1, :] * shifted
        xbc = _silu(conv)

        dt = _softplus(sm_ref[rows, :] + brow_ref[...])
        d_a = dt * (-jnp.exp(arow_ref[...]))
        acum = jnp.dot(tril.astype(F32), d_a, precision=HIGHEST, preferred_element_type=F32)

        for g in range(S_GROUPS):
            xg = xbc[:, g * gw:(g + 1) * gw]
            bg = xbc[:, S_DINNER + g * S_DSTATE:S_DINNER + (g + 1) * S_DSTATE]
            co = S_DINNER + (S_GROUPS + g) * S_DSTATE
            cgb = xbc[:, co:co + S_DSTATE].astype(BF16)
            cb = _nt_dot(cgb, bg.astype(BF16))
            e_g = e_ref[g]
            dt_e = jnp.dot(dt, e_g, precision=HIGHEST, preferred_element_type=F32)
            acum_e = jnp.dot(acum, e_g, precision=HIGHEST, preferred_element_type=F32)
            xc = xg * dt_e
            xcb = xc.astype(BF16)
            last = acum_e[L - 1:L, :]
            ys = []
            for hh in range(S_HPG):
                lane = SM_DT + g * S_HPG + hh
                ac = jnp.broadcast_to(acum[:, lane:lane + 1], (L, L))
                dec = jnp.exp(jnp.where(tril, ac - ac.T, NEG_INF))
                ys.append(jnp.dot((cb * dec).astype(BF16),
                                  xcb[:, hh * S_HEADDIM:(hh + 1) * S_HEADDIM],
                                  preferred_element_type=F32))
            y = jnp.concatenate(ys, axis=1)
            st = st_ref[g]
            y = y + jnp.dot(cgb, st.astype(BF16), preferred_element_type=F32) * jnp.exp(acum_e)
            upd = jnp.dot(bg.T.astype(BF16), (jnp.exp(last - acum_e) * xc).astype(BF16),
                          preferred_element_type=F32)
            st_ref[g] = jnp.exp(last) * st + upd
            y = y + drow_ref[:, g * gw:(g + 1) * gw] * xg
            y = y * _silu(cur_ref[rows, g * gw:(g + 1) * gw])
            y = y * lax.rsqrt(jnp.mean(y * y, -1, keepdims=True) + NORM_EPS)
            o_ref[rows, g * gw:(g + 1) * gw] = (
                y * nw_ref[:, g * gw:(g + 1) * gw]).astype(o_ref.dtype)


def _ssd(sp, sm, brow, arow, drow, cw, cb, nw, e_mat, B, S):
    R = STEP_CHUNKS * CHUNK
    nc = S // R
    W = sp.shape[1]
    const = lambda b, c: (0, 0)
    return pl.pallas_call(
        _ssd_kernel,
        grid=(B, nc),
        in_specs=[pl.BlockSpec((R, W), lambda b, c: (b * nc + c, 0)),
                  pl.BlockSpec((CHUNK, W),
                               lambda b, c: (STEP_CHUNKS * (b * nc + c) - jnp.where(c > 0, 1, 0), 0)),
                  pl.BlockSpec((R, LANES), lambda b, c: (b * nc + c, 0)),
                  pl.BlockSpec((1, LANES), const),
                  pl.BlockSpec((1, LANES), const),
                  pl.BlockSpec((1, S_DINNER), const),
                  pl.BlockSpec((S_CONV, S_CONV_CH), const),
                  pl.BlockSpec((1, S_CONV_CH), const),
                  pl.BlockSpec((1, S_DINNER), const),
                  pl.BlockSpec((S_GROUPS, LANES, S_HPG * S_HEADDIM), lambda b, c: (0, 0, 0))],
        out_specs=pl.BlockSpec((R, S_DINNER), lambda b, c: (b * nc + c, 0)),
        out_shape=jax.ShapeDtypeStruct((B * S, S_DINNER), BF16),
        scratch_shapes=[pltpu.VMEM((S_GROUPS, S_DSTATE, S_HPG * S_HEADDIM), F32)],
        compiler_params=_params("parallel", "arbitrary"),
        name="ssd",
    )(sp, sp, sm, brow, arow, drow, cw, cb, nw, e_mat)


def _rope(x, cos, sin):
    lane = lax.broadcasted_iota(jnp.int32, x.shape, 1)
    first_half = (lane % A_HEADDIM) < (A_HEADDIM // 2)
    rot = jnp.where(first_half, -pltpu.roll(x, LANES - A_HEADDIM // 2, 1),
                    pltpu.roll(x, A_HEADDIM // 2, 1))
    return x * cos + rot * sin


def _swa_kernel(cur_ref, prev_ref, cos_ref, sin_ref, cosp_ref, sinp_ref, sink_ref, o_ref):
    L = CHUNK
    n = pl.program_id(1)
    qo, ko, vo = 0, A_QHEADS * A_HEADDIM, (A_QHEADS + A_KVHEADS) * A_HEADDIM
    vw = A_KVHEADS * A_HEADDIM

    def roped_keys(ref, rows, cos, sin):
        return jnp.concatenate([_rope(ref[rows, ko + j * LANES:ko + (j + 1) * LANES], cos, sin)
                                for j in range(2)], axis=1).astype(BF16)

    everything = slice(None)
    keys = [roped_keys(prev_ref, everything, cosp_ref[...], sinp_ref[...])]
    vals = [prev_ref[:, vo:vo + vw]]
    for cc in range(STEP_CHUNKS):
        rows = slice(cc * L, (cc + 1) * L)
        keys.append(roped_keys(cur_ref, rows, cos_ref[rows, :], sin_ref[rows, :]))
        vals.append(cur_ref[rows, vo:vo + vw])
    krow = lax.broadcasted_iota(jnp.int32, (2 * L, L), 0)
    qcol = lax.broadcasted_iota(jnp.int32, (2 * L, L), 1)
    band = jnp.logical_and(krow > qcol, krow <= qcol + L)
    scale = A_HEADDIM ** -0.5
    for cc in range(STEP_CHUNKS):
        rows = slice(cc * L, (cc + 1) * L)
        cos, sin = cos_ref[rows, :], sin_ref[rows, :]
        kk = jnp.concatenate([keys[cc], keys[cc + 1]], axis=0)
        vv_t = jnp.concatenate([vals[cc], vals[cc + 1]], axis=0).T
        valid = jnp.logical_and(band, krow >= jnp.where(n > 0, 0, L)) if cc == 0 else band
        for j in range(A_QHEADS // 2):
            qpair = _rope(cur_ref[rows, qo + j * LANES:qo + (j + 1) * LANES], cos, sin).astype(BF16)
            outs = []
            for t in range(2):
                hq = 2 * j + t
                g = hq // A_REP
                sl = slice(g * A_HEADDIM, (g + 1) * A_HEADDIM)
                qh = qpair[:, t * A_HEADDIM:(t + 1) * A_HEADDIM]
                s_t = jnp.where(valid, _nt_dot(kk[:, sl], qh) * scale, NEG_INF)
                sink = sink_ref[:, hq:hq + 1]
                m = jnp.maximum(jnp.max(s_t, axis=0, keepdims=True), sink)
                p = jnp.exp(s_t - m)
                den = jnp.sum(p, axis=0, keepdims=True) + jnp.exp(sink - m)
                o_t = jnp.dot(vv_t[sl, :].astype(BF16), p.astype(BF16),
                              preferred_element_type=F32)
                outs.append(o_t / den)
            o_ref[rows, j * LANES:(j + 1) * LANES] = (
                jnp.concatenate(outs, axis=0).T.astype(o_ref.dtype))


def _swa(ap, cos_t, sin_t, sink_row, B, S):
    R = STEP_CHUNKS * CHUNK
    nb = S // R
    W = ap.shape[1]
    cur = lambda b, n: (b * nb + n, 0)
    prev = lambda b, n: (STEP_CHUNKS * (b * nb + n) - jnp.where(n > 0, 1, 0), 0)
    tab = lambda b, n: (n, 0)
    tabp = lambda b, n: (jnp.maximum(STEP_CHUNKS * n - 1, 0), 0)
    return pl.pallas_call(
        _swa_kernel,
        grid=(B, nb),
        in_specs=[pl.BlockSpec((R, W), cur), pl.BlockSpec((CHUNK, W), prev),
                  pl.BlockSpec((R, LANES), tab), pl.BlockSpec((R, LANES), tab),
                  pl.BlockSpec((CHUNK, LANES), tabp), pl.BlockSpec((CHUNK, LANES), tabp),
                  pl.BlockSpec((1, LANES), lambda b, n: (0, 0))],
        out_specs=pl.BlockSpec((R, BRANCH_WIDTH), cur),
        out_shape=jax.ShapeDtypeStruct((B * S, BRANCH_WIDTH), BF16),
        compiler_params=_params("parallel", "parallel"),
        name="swa",
    )(ap, ap, cos_t, sin_t, cos_t, sin_t, sink_row)


def _merge_kernel(ym_ref, ys_ref, ya_ref, g0_ref, g1_ref, g2_ref, b0_ref, b1_ref, b2_ref,
                  w0_ref, w1_ref, w2_ref, o_ref):
    acc = None
    for y_ref, g_ref, b_ref, w_ref in ((ym_ref, g0_ref, b0_ref, w0_ref),
                                       (ys_ref, g1_ref, b1_ref, w1_ref),
                                       (ya_ref, g2_ref, b2_ref, w2_ref)):
        t = _sigmoid(g_ref[...] + b_ref[...]) * jnp.dot(y_ref[...], w_ref[0],
                                                        preferred_element_type=F32)
        acc = t if acc is None else acc + t
    o_ref[...] = acc.astype(o_ref.dtype)


def _merge(ym, ys, ya, gp, gb, wb, tm, tn):
    T = ym.shape[0]
    D = wb.shape[2]
    tm, tn = min(tm, T), min(tn, D)
    nj = D // tn
    yspec = pl.BlockSpec((tm, BRANCH_WIDTH), lambda i, j: (i, 0))
    gspec = lambda k: pl.BlockSpec((tm, tn), lambda i, j: (i, k * nj + j))
    bspec = lambda k: pl.BlockSpec((1, tn), lambda i, j: (0, k * nj + j))
    wspec = lambda k: pl.BlockSpec((1, BRANCH_WIDTH, tn), lambda i, j: (k, 0, j))
    return pl.pallas_call(
        _merge_kernel,
        grid=(T // tm, nj),
        in_specs=[yspec, yspec, yspec, gspec(0), gspec(1), gspec(2), bspec(0), bspec(1), bspec(2),
                  wspec(0), wspec(1), wspec(2)],
        out_specs=pl.BlockSpec((tm, tn), lambda i, j: (i, j)),
        out_shape=jax.ShapeDtypeStruct((T, D), BF16),
        compiler_params=_params("parallel", "arbitrary"),
        name="merge",
    )(ym, ys, ya, gp, gp, gp, gb, gb, gb, wb, wb, wb)


def _outproj_kernel(x_ref, mix_ref, w_ref, g_ref, b_ref, o_ref, ob_ref, *, alpha):
    y = alpha * x_ref[...] + jnp.dot(mix_ref[...], w_ref[...], preferred_element_type=F32)
    r = _layer_norm(y, g_ref[...], b_ref[...])
    o_ref[...] = r
    ob_ref[...] = r.astype(BF16)


def _outproj(x, mix, w, g, b, alpha, tm):
    T, D = x.shape
    tm = min(tm, T)
    rowspec = pl.BlockSpec((tm, D), lambda i: (i, 0))
    vec = pl.BlockSpec((1, D), lambda i: (0, 0))
    return pl.pallas_call(
        functools.partial(_outproj_kernel, alpha=alpha),
        grid=(T // tm,),
        in_specs=[rowspec, rowspec, pl.BlockSpec((D, D), lambda i: (0, 0)), vec, vec],
        out_specs=[rowspec, rowspec],
        out_shape=[jax.ShapeDtypeStruct((T, D), F32), jax.ShapeDtypeStruct((T, D), BF16)],
        compiler_params=_params("parallel"),
        name="outproj_ln",
    )(x, mix, w, g, b)


def _res_ln_kernel(x_ref, yt_ref, g_ref, b_ref, o_ref, ob_ref, *, alpha):
    r = _layer_norm(alpha * x_ref[...] + yt_ref[0].T, g_ref[...], b_ref[...])
    o_ref[...] = r
    ob_ref[...] = r.astype(BF16)


def _res_ln(x, yt, g, b, alpha):
    T, D = x.shape
    tm = yt.shape[2]
    rowspec = pl.BlockSpec((tm, D), lambda i: (i, 0))
    vec = pl.BlockSpec((1, D), lambda i: (0, 0))
    return pl.pallas_call(
        functools.partial(_res_ln_kernel, alpha=alpha),
        grid=(T // tm,),
        in_specs=[rowspec, pl.BlockSpec((1, D, tm), lambda i: (i, 0, 0)), vec, vec],
        out_specs=[rowspec, rowspec],
        out_shape=[jax.ShapeDtypeStruct((T, D), F32), jax.ShapeDtypeStruct((T, D), BF16)],
        compiler_params=_params("parallel"),
        name="res_ln",
    )(x, yt, g, b)


_CAND_NQ = (16, 8, 5, 4, 3, 2, 2, 2)
_CAND_ROWS = 16 + 7 * 8 + 8


def _extract_top(work, iota, n_iter, on_pick):
    big = float(work.shape[0])
    for p in range(n_iter):
        m = jnp.max(work, axis=0, keepdims=True)
        idx = jnp.min(jnp.where(work == m, iota, big), axis=0, keepdims=True)
        onehot = iota == idx
        work = jnp.where(onehot, NEG_INF, work)
        on_pick(p, m, onehot)


def _select_chunk(s1, s2, top_ref, cnt_ref, lanes, exact):
    K = P_TOPK
    n = s1.shape[1]
    iota = lax.broadcasted_iota(jnp.int32, (P_NKEYS, n), 0).astype(F32)
    ranks = []
    suspect = jnp.zeros((1, n), F32)
    for c, s in enumerate((s1, s2)):
        rank = jnp.full((P_NKEYS, n), float(K), F32)
        if exact:
            rank_holder = [rank]

            def on_pick(p, m, onehot, c=c, rank_holder=rank_holder):
                top_ref[c, p:p + 1, lanes] = m
                rank_holder[0] = jnp.where(onehot, float(p), rank_holder[0])

            _extract_top(s, iota, K, on_pick)
            rank = rank_holder[0]
        else:
            work = s
            for p in range(K):
                m = jnp.max(work, axis=0, keepdims=True)
                hit = work == m
                work = jnp.where(hit, NEG_INF, work)
                rank = jnp.where(hit, float(p), rank)
                top_ref[c, p:p + 1, lanes] = m
            n_ranked = jnp.sum(jnp.where(rank < float(K), 1.0, 0.0), axis=0, keepdims=True)
            suspect = suspect + jnp.abs(n_ranked - float(K))
        ranks.append(rank)

    a = top_ref[0, :, lanes]
    b = top_ref[1, :, lanes]
    ea = jnp.exp(a - a[0:1])
    eb = jnp.exp(b - b[0:1])
    q8 = lax.broadcasted_iota(jnp.int32, (8, n), 0)
    cand = [a[0:1] + b]
    wcand = [ea[0:1] * eb]
    for p in range(1, 8):
        ok = q8 < _CAND_NQ[p]
        cand.append(jnp.where(ok, a[p:p + 1] + b[0:8], NEG_INF))
        wcand.append(ea[p:p + 1] * eb[0:8])
    cand.append(a[8:16] + b[0:1])
    wcand.append(ea[8:16] * eb[0:1])
    cand = jnp.concatenate(cand, axis=0)
    wcand = jnp.concatenate(wcand, axis=0)
    if exact:
        iota_c = lax.broadcasted_iota(jnp.int32, (_CAND_ROWS, n), 0).astype(F32)
        sel_holder = [jnp.zeros((_CAND_ROWS, n), F32)]

        def on_pick2(p, m, onehot):
            sel_holder[0] = jnp.where(onehot, 1.0, sel_holder[0])

        _extract_top(cand, iota_c, K, on_pick2)
        sel = sel_holder[0]
    else:
        work = cand
        for p in range(K):
            m = jnp.max(work, axis=0, keepdims=True)
            work = jnp.where(work == m, NEG_INF, work)
        sel = jnp.where(work != cand, 1.0, 0.0)
        suspect = suspect + jnp.abs(jnp.sum(sel, axis=0, keepdims=True) - float(K))
    z = jnp.sum(sel * wcand, axis=0, keepdims=True)
    cnt_ref[0:1, lanes] = jnp.sum(sel[0:16], axis=0, keepdims=True)
    for p in range(1, 8):
        cnt_ref[p:p + 1, lanes] = jnp.sum(sel[8 + 8 * p:16 + 8 * p], axis=0, keepdims=True)
    cnt_ref[8:16, lanes] = sel[72:80]
    cnt = cnt_ref[:, lanes]
    cnt1 = jnp.zeros((P_NKEYS, n), F32)
    for p in range(K):
        cnt1 = jnp.where(ranks[0] == float(p), cnt[p:p + 1], cnt1)
    return ranks[1], cnt1, jnp.exp(s1 - a[0:1]), jnp.exp(s2 - b[0:1]) / z, suspect


def _peer_select_kernel(xb_ref, wq_ref, sk_ref, rank2_ref, cnt1_ref, e1_ref, e2_ref,
                        q_scr, s_scr, top_scr, cnt_scr):
    h = pl.program_id(1)
    tm = xb_ref.shape[0]

    @pl.when(h == 0)
    def _():
        q = jnp.dot(xb_ref[...], wq_ref[...], preferred_element_type=F32)
        for hc in range(2 * P_HEADS):
            q_scr[hc] = q[:, hc * LANES:(hc + 1) * LANES]

    for c in range(2):
        s_scr[c] = _nt_dot(sk_ref[0, c], q_scr[2 * h + c], precision=HIGHEST)

    def select_tile(exact):
        suspect = jnp.zeros((1, LANES), F32)
        for lc in range(tm // LANES):
            lanes = slice(lc * LANES, (lc + 1) * LANES)
            rank2, cnt1, e1, e2, sus = _select_chunk(s_scr[0, :, lanes], s_scr[1, :, lanes],
                                                     top_scr, cnt_scr, lanes, exact)
            rank2_ref[0, :, lanes] = rank2.astype(rank2_ref.dtype)
            cnt1_ref[0, :, lanes] = cnt1
            e1_ref[0, :, lanes] = e1
            e2_ref[0, :, lanes] = e2.astype(e2_ref.dtype)
            suspect = suspect + sus
        return suspect

    suspect = select_tile(exact=False)

    @pl.when(jnp.max(suspect) > 0.0)
    def _():
        select_tile(exact=True)


def _peer_select(xb, wq, sk, tm):
    T, D = xb.shape
    tm = min(tm, T)
    out = jax.ShapeDtypeStruct((P_HEADS, P_NKEYS, T), F32)
    outb = jax.ShapeDtypeStruct((P_HEADS, P_NKEYS, T), BF16)
    ospec = pl.BlockSpec((1, P_NKEYS, tm), lambda i, h: (h, 0, i))
    return pl.pallas_call(
        _peer_select_kernel,
        grid=(T // tm, P_HEADS),
        in_specs=[pl.BlockSpec((tm, D), lambda i, h: (i, 0)),
                  pl.BlockSpec(wq.shape, lambda i, h: (0, 0)),
                  pl.BlockSpec((1, 2, P_NKEYS, LANES), lambda i, h: (h, 0, 0, 0))],
        out_specs=[ospec, ospec, ospec, ospec],
        out_shape=[outb, out, out, outb],
        scratch_shapes=[pltpu.VMEM((2 * P_HEADS, tm, LANES), F32),
                        pltpu.VMEM((2, P_NKEYS, tm), F32),
                        pltpu.VMEM((2, P_TOPK, tm), F32),
                        pltpu.VMEM((P_TOPK, tm), F32)],
        compiler_params=_params("parallel", "arbitrary"),
        name="peer_select",
    )(xb, wq, sk)


def _peer_gated_act(act_t, blk, rank2_ref, cnt1_ref, e1_ref, e2_ref, nsub):
    parts = []
    for ii in range(nsub):
        i = blk * nsub + ii
        w = None
        for h in range(P_HEADS):
            cnt = cnt1_ref[h, pl.ds(i, 1), :].astype(BF16)
            e1 = e1_ref[h, pl.ds(i, 1), :].astype(BF16)
            t = jnp.where(rank2_ref[h] < cnt, e2_ref[h], jnp.zeros((), BF16)) * e1
            w = t if w is None else w + t
        a = act_t[ii * P_NKEYS:(ii + 1) * P_NKEYS]
        gelu = 0.5 * a * (1.0 + lax.erf(a * (0.5 ** 0.5)))
        parts.append(w * gelu.astype(BF16))
    return jnp.concatenate(parts, axis=0) if nsub > 1 else parts[0]


def _peer_dense_kernel(xb_ref, rank2_ref, cnt1_ref, e1_ref, e2_ref, u_ref, *rest, nsub, pairs):
    vt_refs, (o_ref, pa_ref, pb_ref) = rest[:2 * pairs], rest[2 * pairs:]
    s = pl.program_id(1)
    ns = pl.num_programs(1) - 1
    sub = nsub * P_NKEYS
    sel = (rank2_ref, cnt1_ref, e1_ref, e2_ref)

    @pl.when(s == 0)
    def _():
        o_ref[0] = jnp.zeros(o_ref.shape[1:], F32)
        pa_ref[...] = jnp.zeros_like(pa_ref)

    @pl.when(s < ns)
    def _():
        xb = xb_ref[...]
        for r in range(pairs):
            blk = 2 * (pairs * s + r)
            act0 = _nt_dot(u_ref[2 * r * sub:(2 * r + 1) * sub, :], xb)
            act1 = _nt_dot(u_ref[(2 * r + 1) * sub:(2 * r + 2) * sub, :], xb)
            o_ref[0] += jnp.dot(vt_refs[2 * r][0], pa_ref[...], preferred_element_type=F32)
            pb_ref[...] = _peer_gated_act(act0, blk, *sel, nsub)
            o_ref[0] += jnp.dot(vt_refs[2 * r + 1][0], pb_ref[...], preferred_element_type=F32)
            pa_ref[...] = _peer_gated_act(act1, blk + 1, *sel, nsub)

    @pl.when(s == ns)
    def _():
        o_ref[0] += jnp.dot(vt_refs[0][0], pa_ref[...], preferred_element_type=F32)


def _peer_dense(xb, rank2, cnt1, e1, e2, u_b, vt_b, tm, pairs=2):
    T, D = xb.shape
    tm = min(tm, T)
    nblk, _, be = vt_b.shape
    nsub = be // P_NKEYS
    per_step = 2 * pairs
    ns = nblk // per_step
    sel = pl.BlockSpec((P_HEADS, P_NKEYS, tm), lambda i, s: (0, 0, i))

    def vt_spec(k):
        return pl.BlockSpec((1, D, be),
                            lambda i, s: (jnp.clip(per_step * s - 1 + k, 0, nblk - 1), 0, 0))

    return pl.pallas_call(
        functools.partial(_peer_dense_kernel, nsub=nsub, pairs=pairs),
        grid=(T // tm, ns + 1),
        in_specs=[pl.BlockSpec((tm, D), lambda i, s: (i, 0)), sel, sel, sel, sel,
                  pl.BlockSpec((per_step * be, D), lambda i, s: (jnp.minimum(s, ns - 1), 0))]
                 + [vt_spec(k) for k in range(per_step)],
        out_specs=pl.BlockSpec((1, D, tm), lambda i, s: (i, 0, 0)),
        out_shape=jax.ShapeDtypeStruct((T // tm, D, tm), F32),
        scratch_shapes=[pltpu.VMEM((be, tm), BF16), pltpu.VMEM((be, tm), BF16)],
        compiler_params=_params("parallel", "arbitrary"),
        name="peer_dense",
    )(xb, rank2, cnt1, e1, e2, u_b, *([vt_b] * per_step))


def _pad_row(vals_at, width=LANES):
    row = jnp.zeros((1, width), F32)
    for off, v in vals_at:
        row = row.at[0, off:off + v.shape[0]].set(v.astype(F32))
    return row


def _rope_tables(S):
    half = A_HEADDIM // 2
    freqs = ROPE_THETA ** (-jnp.arange(half, dtype=F32) / half)
    ang = jnp.arange(S, dtype=F32)[:, None] * freqs[None, :]
    reps = LANES // half
    return jnp.tile(jnp.cos(ang), (1, reps)), jnp.tile(jnp.sin(ang), (1, reps))


def _head_expand_matrix():
    e = np.zeros((S_GROUPS, LANES, S_HPG * S_HEADDIM), np.float32)
    for g in range(S_GROUPS):
        for hh in range(S_HPG):
            e[g, SM_DT + g * S_HPG + hh, hh * S_HEADDIM:(hh + 1) * S_HEADDIM] = 1.0
    return jnp.asarray(e)


def _layer(x, xb, p, consts, B, S, alpha):
    T, D = x.shape
    w_in = p["w_in"]
    o_m = 2 * M_HEADS * M_DQK + 2 * M_HEADS * M_DV
    o_s = o_m + 2 * M_HEADS
    o_dt = o_s + S_DINNER + S_CONV_CH
    o_a = o_dt + S_HEADS
    o_g = o_a + (A_QHEADS + 2 * A_KVHEADS) * A_HEADDIM
    w_m = w_in[:, :o_m].astype(BF16)
    w_s = w_in[:, o_s:o_dt].astype(BF16)
    w_a = w_in[:, o_a:o_g].astype(BF16)
    w_g = w_in[:, o_g:].astype(BF16)
    w_sm = jnp.concatenate([w_in[:, o_m:o_s], w_in[:, o_dt:o_a],
                            jnp.zeros((D, LANES - 2 * M_HEADS - S_HEADS), F32)], axis=1).astype(BF16)

    mp = _matmul(xb, w_m, 1024, 1536)
    sp = _matmul(xb, w_s, 1024, 1280)
    ap = _matmul(xb, w_a, 1024, 1536)
    gp = _matmul(xb, w_g, 1024, 2048)
    sm = _matmul(xb, w_sm, 1024, LANES)

    brow = _pad_row([(SM_I, p["mlstm_gate_b"][0]), (SM_F, p["mlstm_gate_b"][1]),
                     (SM_DT, p["ssm_dt_bias"])])
    arow = _pad_row([(SM_DT, p["ssm_a_log"])])
    drow = jnp.repeat(p["ssm_d"].astype(F32), S_HEADDIM)[None, :]
    y_m = _mlstm(mp, sm, brow, p["mlstm_norm_w"][None, :], B, S)
    y_s = _ssd(sp, sm, brow, arow, drow, p["ssm_conv_w"][:, 0, :], p["ssm_conv_b"][None, :],
               p["ssm_norm_w"][None, :], consts["e_mat"], B, S)
    y_a = _swa(ap, consts["cos"], consts["sin"], _pad_row([(0, p["swa_sinks"])]), B, S)

    mix = _merge(y_m, y_s, y_a, gp, p["merge_gate_b"].reshape(1, 3 * D),
                 p["w_branch"].astype(BF16), 1024, 512)
    x1, x1b = _outproj(x, mix, p["w_out"].astype(BF16), p["ln1_g"][None, :], p["ln1_b"][None, :],
                       alpha, 512)

    rank2, cnt1, e1, e2 = _peer_select(x1b, p["peer_wq"].astype(BF16), p["peer_subkeys"], 512)
    be = 256
    vt_blocks = jnp.swapaxes(p["peer_v"].astype(BF16).reshape(P_EXPERTS // be, be, D), 1, 2)
    peer = _peer_dense(x1b, rank2, cnt1, e1, e2, p["peer_u"].astype(BF16), vt_blocks, 512)
    return _res_ln(x1, peer, p["ln2_g"][None, :], p["ln2_b"][None, :], alpha)


def _forward(x, params, depth):
    B, S, D = x.shape
    alpha = (2.0 * depth) ** 0.25
    cos_t, sin_t = _rope_tables(S)
    consts = {"cos": cos_t, "sin": sin_t, "e_mat": _head_expand_matrix()}
    xf = x.reshape(B * S, D)

    def body(carry, p):
        xc, xcb = carry
        return _layer(xc, xcb, p, consts, B, S, alpha), None

    (xf, _), _ = lax.scan(body, (xf, xf.astype(BF16)), params)
    return xf.reshape(B, S, D)


def kernel(x, w_in, mlstm_gate_b, mlstm_norm_w, ssm_conv_w, ssm_conv_b, ssm_dt_bias, ssm_a_log,
           ssm_d, ssm_norm_w, swa_sinks, merge_gate_b, w_branch, w_out, ln1_g, ln1_b,
           peer_wq, peer_subkeys, peer_u, peer_v, ln2_g, ln2_b):
    params = dict(w_in=w_in, mlstm_gate_b=mlstm_gate_b, mlstm_norm_w=mlstm_norm_w,
                  ssm_conv_w=ssm_conv_w, ssm_conv_b=ssm_conv_b, ssm_dt_bias=ssm_dt_bias,
                  ssm_a_log=ssm_a_log, ssm_d=ssm_d, ssm_norm_w=ssm_norm_w, swa_sinks=swa_sinks,
                  merge_gate_b=merge_gate_b, w_branch=w_branch, w_out=w_out, ln1_g=ln1_g,
                  ln1_b=ln1_b, peer_wq=peer_wq, peer_subkeys=peer_subkeys, peer_u=peer_u,
                  peer_v=peer_v, ln2_g=ln2_g, ln2_b=ln2_b)
    return _forward(x, params, w_in.shape[0])
```

```python
import functools
import math

import jax
import jax.numpy as jnp
import numpy as np
from jax import lax
from jax.experimental import pallas as pl
from jax.experimental.pallas import tpu as pltpu

F32 = jnp.float32
BF16 = jnp.bfloat16
HIGHEST = lax.Precision.HIGHEST

LN_EPS = 1e-5
NORM_EPS = 1e-6

LANES = 128
CHUNK = 128
STEP_CHUNKS = 4
SELECT_HEADS = 2
VMEM_LIMIT = 56 * 1024 * 1024

M_HEADS, M_DQK, M_DV = 4, 128, 256
S_HEADS, S_GROUPS, S_HPG, S_HEADDIM, S_DSTATE, S_CONV = 16, 2, 8, 64, 128, 4
S_DINNER = 1024
S_CONV_CH = S_DINNER + 2 * S_GROUPS * S_DSTATE
A_QHEADS, A_KVHEADS, A_HEADDIM, A_REP = 16, 4, 64, 4
ROPE_THETA = 10000.0
P_HEADS, P_NKEYS, P_TOPK = 8, 128, 16
P_EXPERTS = P_NKEYS * P_NKEYS
BRANCH_WIDTH = 1024

SM_I, SM_F, SM_DT = 0, M_HEADS, 2 * M_HEADS

NEG_INF = float("-inf")


def _params(*sem, flags=None):
    return pltpu.CompilerParams(dimension_semantics=sem, vmem_limit_bytes=VMEM_LIMIT, flags=flags)


def _nt_dot(a, b, **kw):
    return lax.dot_general(a, b, (((1,), (1,)), ((), ())), preferred_element_type=F32, **kw)


def _softplus(x):
    return jnp.maximum(x, 0.0) + jnp.log(1.0 + jnp.exp(-jnp.abs(x)))


def _log_sigmoid(x):
    return jnp.minimum(x, 0.0) - jnp.log(1.0 + jnp.exp(-jnp.abs(x)))


def _sigmoid(x):
    return 1.0 / (1.0 + jnp.exp(-x))


def _silu(x):
    return x * _sigmoid(x)


def _layer_norm(y, g, b):
    mu = jnp.mean(y, -1, keepdims=True)
    d = y - mu
    var = jnp.mean(d * d, -1, keepdims=True)
    return d * lax.rsqrt(var + LN_EPS) * g + b


def _mm_kernel(x_ref, w_ref, o_ref):
    o_ref[...] = jnp.dot(x_ref[...], w_ref[...], preferred_element_type=F32).astype(o_ref.dtype)


def _matmul(x, w, tm, tn, out_dtype=F32):
    T, K = x.shape
    N = w.shape[1]
    tm, tn = min(tm, T), min(tn, N)
    return pl.pallas_call(
        _mm_kernel,
        grid=(T // tm, N // tn),
        in_specs=[pl.BlockSpec((tm, K), lambda i, j: (i, 0)),
                  pl.BlockSpec((K, tn), lambda i, j: (0, j))],
        out_specs=pl.BlockSpec((tm, tn), lambda i, j: (i, j)),
        out_shape=jax.ShapeDtypeStruct((T, N), out_dtype),
        compiler_params=_params("parallel", "arbitrary"),
        name="proj_matmul",
    )(x, w)


def _mlstm_kernel(mp_ref, sm_ref, brow_ref, nw_ref, o_ref, c_ref, n_ref, m_ref):
    L = CHUNK

    @pl.when(pl.program_id(1) == 0)
    def _():
        c_ref[...] = jnp.zeros_like(c_ref)
        n_ref[...] = jnp.zeros_like(n_ref)
        m_ref[...] = jnp.zeros_like(m_ref)

    row = lax.broadcasted_iota(jnp.int32, (L, L), 0)
    col = lax.broadcasted_iota(jnp.int32, (L, L), 1)
    tril = row >= col
    trilf = tril.astype(F32)
    scale = M_DQK ** -0.5
    for cc, h in [(cc, h) for cc in range(STEP_CHUNKS) for h in range(M_HEADS)]:
        rows = slice(cc * L, (cc + 1) * L)
        gates = sm_ref[rows, :] + brow_ref[...]
        q = mp_ref[rows, h * M_DQK:(h + 1) * M_DQK]
        k = mp_ref[rows, 512 + h * M_DQK:512 + (h + 1) * M_DQK] * scale
        v = mp_ref[rows, 1024 + h * M_DV:1024 + (h + 1) * M_DV].astype(BF16)
        og = mp_ref[rows, 2048 + h * M_DV:2048 + (h + 1) * M_DV]
        li_col = gates[:, SM_I + h:SM_I + h + 1]
        lf_col = _log_sigmoid(gates[:, SM_F + h:SM_F + h + 1])
        bc = jnp.dot(trilf, jnp.broadcast_to(lf_col, (L, L)), precision=HIGHEST,
                     preferred_element_type=F32)
        br = bc.T
        lir = jnp.broadcast_to(li_col, (L, L)).T
        b_col = bc[:, 0:1]
        m_prev = m_ref[h, 0:1, 0:1]
        log_d = jnp.where(tril, bc - br + lir, NEG_INF)
        m_t = jnp.maximum(jnp.max(log_d, axis=1, keepdims=True), b_col + m_prev)
        w_intra = jnp.exp(log_d - m_t)
        w_inter = jnp.exp(b_col + m_prev - m_t)
        qb = q.astype(BF16)
        s = _nt_dot(qb, k.astype(BF16)) * w_intra
        c_old = c_ref[h]
        n_old = n_ref[h, 0:1, :]
        num = (jnp.dot(s.astype(BF16), v, preferred_element_type=F32)
               + w_inter * jnp.dot(qb, c_old.astype(BF16), preferred_element_type=F32))
        den = (jnp.sum(s, axis=1, keepdims=True)
               + w_inter * jnp.sum(q * n_old, axis=1, keepdims=True))
        hh = num / jnp.maximum(jnp.abs(den), jnp.exp(-m_t))

        g = bc[L - 1:L, 0:1]
        lws = g - b_col + li_col
        m_new = jnp.maximum(g + m_prev, jnp.max(lws, axis=0, keepdims=True))
        a_prev = jnp.exp(g + m_prev - m_new)
        kw = k * jnp.exp(lws - m_new)
        c_ref[h] = a_prev * c_old + jnp.dot(kw.T.astype(BF16), v, preferred_element_type=F32)
        n_ref[h, 0:1, :] = a_prev * n_old + jnp.sum(kw, axis=0, keepdims=True)
        m_ref[h] = jnp.broadcast_to(m_new, m_ref.shape[1:])

        mu = jnp.mean(hh, -1, keepdims=True)
        d = hh - mu
        var = jnp.mean(d * d, -1, keepdims=True)
        hn = d * lax.rsqrt(var + NORM_EPS) * nw_ref[:, h * M_DV:(h + 1) * M_DV]
        o_ref[rows, h * M_DV:(h + 1) * M_DV] = (_sigmoid(og) * hn).astype(o_ref.dtype)


def _mlstm(mp, sm, brow, nw, B, S):
    R = STEP_CHUNKS * CHUNK
    nc = S // R
    W = mp.shape[1]
    return pl.pallas_call(
        _mlstm_kernel,
        grid=(B, nc),
        in_specs=[pl.BlockSpec((R, W), lambda b, c: (b * nc + c, 0)),
                  pl.BlockSpec((R, LANES), lambda b, c: (b * nc + c, 0)),
                  pl.BlockSpec((1, LANES), lambda b, c: (0, 0)),
                  pl.BlockSpec((1, BRANCH_WIDTH), lambda b, c: (0, 0))],
        out_specs=pl.BlockSpec((R, BRANCH_WIDTH), lambda b, c: (b * nc + c, 0)),
        out_shape=jax.ShapeDtypeStruct((B * S, BRANCH_WIDTH), BF16),
        scratch_shapes=[pltpu.VMEM((M_HEADS, M_DQK, M_DV), F32),
                        pltpu.VMEM((M_HEADS, 8, M_DQK), F32),
                        pltpu.VMEM((M_HEADS, 8, LANES), F32)],
        compiler_params=_params("parallel", "arbitrary"),
        name="mlstm",
    )(mp, sm, brow, nw)


def _ssd_kernel(cur_ref, prev_ref, sm_ref, brow_ref, arow_ref, drow_ref, cw_ref, cb_ref, nw_ref,
                e_ref, o_ref, st_ref):
    L = CHUNK
    c = pl.program_id(1)

    @pl.when(c == 0)
    def _():
        st_ref[...] = jnp.zeros_like(st_ref)

    ridx = lax.broadcasted_iota(jnp.int32, (L, S_CONV_CH), 0)
    row = lax.broadcasted_iota(jnp.int32, (L, L), 0)
    col = lax.broadcasted_iota(jnp.int32, (L, L), 1)
    tril = row >= col
    gw = S_HPG * S_HEADDIM
    for cc in range(STEP_CHUNKS):
        rows = slice(cc * L, (cc + 1) * L)
        u = cur_ref[rows, S_DINNER:]
        if cc == 0:
            up = prev_ref[:, S_DINNER:] * (c > 0).astype(F32)
        else:
            up = cur_ref[(cc - 1) * L:cc * L, S_DINNER:]
        conv = cb_ref[...] + cw_ref[S_CONV - 1:S_CONV, :] * u
        for kk in range(S_CONV - 1):
            sh = S_CONV - 1 - kk
            shifted = jnp.where(ridx < sh, pltpu.roll(up, sh, 0), pltpu.roll(u, sh, 0))
            conv = conv + cw_ref[kk:kk + 1, :] * shifted
        xbc = _silu(conv)

        dt = _softplus(sm_ref[rows, :] + brow_ref[...])
        d_a = dt * (-jnp.exp(arow_ref[...]))
        acum = jnp.dot(tril.astype(F32), d_a, precision=HIGHEST, preferred_element_type=F32)

        for g in range(S_GROUPS):
            xg = xbc[:, g * gw:(g + 1) * gw]
            bg = xbc[:, S_DINNER + g * S_DSTATE:S_DINNER + (g + 1) * S_DSTATE]
            co = S_DINNER + (S_GROUPS + g) * S_DSTATE
            cgb = xbc[:, co:co + S_DSTATE].astype(BF16)
            cb = _nt_dot(cgb, bg.astype(BF16))
            e_g = e_ref[g]
            dt_e = jnp.dot(dt, e_g, precision=HIGHEST, preferred_element_type=F32)
            acum_e = jnp.dot(acum, e_g, precision=HIGHEST, preferred_element_type=F32)
            xc = xg * dt_e
            xcb = xc.astype(BF16)
            last = acum_e[L - 1:L, :]
            ys = []
            for hh in range(S_HPG):
                lane = SM_DT + g * S_HPG + hh
                ac = jnp.broadcast_to(acum[:, lane:lane + 1], (L, L))
                dec = jnp.exp(jnp.where(tril, ac - ac.T, NEG_INF))
                ys.append(jnp.dot((cb * dec).astype(BF16),
                                  xcb[:, hh * S_HEADDIM:(hh + 1) * S_HEADDIM],
                                  preferred_element_type=F32))
            y = jnp.concatenate(ys, axis=1)
            st = st_ref[g]
            y = y + jnp.dot(cgb, st.astype(BF16), preferred_element_type=F32) * jnp.exp(acum_e)
            upd = jnp.dot(bg.T.astype(BF16), (jnp.exp(last - acum_e) * xc).astype(BF16),
                          preferred_element_type=F32)
            st_ref[g] = jnp.exp(last) * st + upd
            y = y + drow_ref[:, g * gw:(g + 1) * gw] * xg
            y = y * _silu(cur_ref[rows, g * gw:(g + 1) * gw])
            y = y * lax.rsqrt(jnp.mean(y * y, -1, keepdims=True) + NORM_EPS)
            o_ref[rows, g * gw:(g + 1) * gw] = (
                y * nw_ref[:, g * gw:(g + 1) * gw]).astype(o_ref.dtype)


def _ssd(sp, sm, brow, arow, drow, cw, cb, nw, e_mat, B, S):
    R = STEP_CHUNKS * CHUNK
    nc = S // R
    W = sp.shape[1]
    const = lambda b, c: (0, 0)
    return pl.pallas_call(
        _ssd_kernel,
        grid=(B, nc),
        in_specs=[pl.BlockSpec((R, W), lambda b, c: (b * nc + c, 0)),
                  pl.BlockSpec((CHUNK, W),
                               lambda b, c: (STEP_CHUNKS * (b * nc + c) - jnp.where(c > 0, 1, 0), 0)),
                  pl.BlockSpec((R, LANES), lambda b, c: (b * nc + c, 0)),
                  pl.BlockSpec((1, LANES), const),
                  pl.BlockSpec((1, LANES), const),
                  pl.BlockSpec((1, S_DINNER), const),
                  pl.BlockSpec((S_CONV, S_CONV_CH), const),
                  pl.BlockSpec((1, S_CONV_CH), const),
                  pl.BlockSpec((1, S_DINNER), const),
                  pl.BlockSpec((S_GROUPS, LANES, S_HPG * S_HEADDIM), lambda b, c: (0, 0, 0))],
        out_specs=pl.BlockSpec((R, S_DINNER), lambda b, c: (b * nc + c, 0)),
        out_shape=jax.ShapeDtypeStruct((B * S, S_DINNER), BF16),
        scratch_shapes=[pltpu.VMEM((S_GROUPS, S_DSTATE, S_HPG * S_HEADDIM), F32)],
        compiler_params=_params("parallel", "arbitrary"),
        name="ssd",
    )(sp, sp, sm, brow, arow, drow, cw, cb, nw, e_mat)


def _rope(x, cos, sin):
    lane = lax.broadcasted_iota(jnp.int32, x.shape, 1)
    first_half = (lane % A_HEADDIM) < (A_HEADDIM // 2)
    rot = jnp.where(first_half, -pltpu.roll(x, LANES - A_HEADDIM // 2, 1),
                    pltpu.roll(x, A_HEADDIM // 2, 1))
    return x * cos + rot * sin


def _swa_kernel(cur_ref, prev_ref, cos_ref, sin_ref, cosp_ref, sinp_ref, sink_ref, o_ref):
    L = CHUNK
    n = pl.program_id(1)
    qo, ko, vo = 0, A_QHEADS * A_HEADDIM, (A_QHEADS + A_KVHEADS) * A_HEADDIM
    vw = A_KVHEADS * A_HEADDIM

    def roped_keys(ref, rows, cos, sin):
        return jnp.concatenate([_rope(ref[rows, ko + j * LANES:ko + (j + 1) * LANES], cos, sin)
                                for j in range(2)], axis=1).astype(BF16)

    everything = slice(None)
    keys = [roped_keys(prev_ref, everything, cosp_ref[...], sinp_ref[...])]
    vals = [prev_ref[:, vo:vo + vw]]
    for cc in range(STEP_CHUNKS):
        rows = slice(cc * L, (cc + 1) * L)
        keys.append(roped_keys(cur_ref, rows, cos_ref[rows, :], sin_ref[rows, :]))
        vals.append(cur_ref[rows, vo:vo + vw])
    krow = lax.broadcasted_iota(jnp.int32, (2 * L, L), 0)
    qcol = lax.broadcasted_iota(jnp.int32, (2 * L, L), 1)
    band = jnp.logical_and(krow > qcol, krow <= qcol + L)
    scale = A_HEADDIM ** -0.5
    for cc in range(STEP_CHUNKS):
        rows = slice(cc * L, (cc + 1) * L)
        cos, sin = cos_ref[rows, :], sin_ref[rows, :]
        kk = jnp.concatenate([keys[cc], keys[cc + 1]], axis=0)
        vv_t = jnp.concatenate([vals[cc], vals[cc + 1]], axis=0).T
        valid = jnp.logical_and(band, krow >= jnp.where(n > 0, 0, L)) if cc == 0 else band
        for j in range(A_QHEADS // 2):
            qpair = _rope(cur_ref[rows, qo + j * LANES:qo + (j + 1) * LANES], cos, sin).astype(BF16)
            outs = []
            for t in range(2):
                hq = 2 * j + t
                g = hq // A_REP
                sl = slice(g * A_HEADDIM, (g + 1) * A_HEADDIM)
                qh = qpair[:, t * A_HEADDIM:(t + 1) * A_HEADDIM]
                s_t = jnp.where(valid, _nt_dot(kk[:, sl], qh) * scale, NEG_INF)
                sink = sink_ref[:, hq:hq + 1]
                m = jnp.maximum(jnp.max(s_t, axis=0, keepdims=True), sink)
                p = jnp.exp(s_t - m)
                den = jnp.sum(p, axis=0, keepdims=True) + jnp.exp(sink - m)
                o_t = jnp.dot(vv_t[sl, :].astype(BF16), p.astype(BF16),
                              preferred_element_type=F32)
                outs.append(o_t / den)
            o_ref[rows, j * LANES:(j + 1) * LANES] = (
                jnp.concatenate(outs, axis=0).T.astype(o_ref.dtype))


def _swa(ap, cos_t, sin_t, sink_row, B, S):
    R = STEP_CHUNKS * CHUNK
    nb = S // R
    W = ap.shape[1]
    cur = lambda b, n: (b * nb + n, 0)
    prev = lambda b, n: (STEP_CHUNKS * (b * nb + n) - jnp.where(n > 0, 1, 0), 0)
    tab = lambda b, n: (n, 0)
    tabp = lambda b, n: (jnp.maximum(STEP_CHUNKS * n - 1, 0), 0)
    return pl.pallas_call(
        _swa_kernel,
        grid=(B, nb),
        in_specs=[pl.BlockSpec((R, W), cur), pl.BlockSpec((CHUNK, W), prev),
                  pl.BlockSpec((R, LANES), tab), pl.BlockSpec((R, LANES), tab),
                  pl.BlockSpec((CHUNK, LANES), tabp), pl.BlockSpec((CHUNK, LANES), tabp),
                  pl.BlockSpec((1, LANES), lambda b, n: (0, 0))],
        out_specs=pl.BlockSpec((R, BRANCH_WIDTH), cur),
        out_shape=jax.ShapeDtypeStruct((B * S, BRANCH_WIDTH), BF16),
        compiler_params=_params("parallel", "parallel"),
        name="swa",
    )(ap, ap, cos_t, sin_t, cos_t, sin_t, sink_row)


def _merge_kernel(ym_ref, ys_ref, ya_ref, g0_ref, g1_ref, g2_ref, b0_ref, b1_ref, b2_ref,
                  w0_ref, w1_ref, w2_ref, o_ref):
    acc = None
    for y_ref, g_ref, b_ref, w_ref in ((ym_ref, g0_ref, b0_ref, w0_ref),
                                       (ys_ref, g1_ref, b1_ref, w1_ref),
                                       (ya_ref, g2_ref, b2_ref, w2_ref)):
        t = _sigmoid(g_ref[...] + b_ref[...]) * jnp.dot(y_ref[...], w_ref[0],
                                                        preferred_element_type=F32)
        acc = t if acc is None else acc + t
    o_ref[...] = acc.astype(o_ref.dtype)


def _merge(ym, ys, ya, gp, gb, wb, tm, tn):
    T = ym.shape[0]
    D = wb.shape[2]
    tm, tn = min(tm, T), min(tn, D)
    nj = D // tn
    yspec = pl.BlockSpec((tm, BRANCH_WIDTH), lambda i, j: (i, 0))
    gspec = lambda k: pl.BlockSpec((tm, tn), lambda i, j: (i, k * nj + j))
    bspec = lambda k: pl.BlockSpec((1, tn), lambda i, j: (0, k * nj + j))
    wspec = lambda k: pl.BlockSpec((1, BRANCH_WIDTH, tn), lambda i, j: (k, 0, j))
    return pl.pallas_call(
        _merge_kernel,
        grid=(T // tm, nj),
        in_specs=[yspec, yspec, yspec, gspec(0), gspec(1), gspec(2), bspec(0), bspec(1), bspec(2),
                  wspec(0), wspec(1), wspec(2)],
        out_specs=pl.BlockSpec((tm, tn), lambda i, j: (i, j)),
        out_shape=jax.ShapeDtypeStruct((T, D), BF16),
        compiler_params=_params("parallel", "arbitrary"),
        name="merge",
    )(ym, ys, ya, gp, gp, gp, gb, gb, gb, wb, wb, wb)


def _outproj_kernel(x_ref, mix_ref, w_ref, g_ref, b_ref, o_ref, ob_ref, *, alpha):
    y = alpha * x_ref[...] + jnp.dot(mix_ref[...], w_ref[...], preferred_element_type=F32)
    r = _layer_norm(y, g_ref[...], b_ref[...])
    o_ref[...] = r
    ob_ref[...] = r.astype(BF16)


def _outproj(x, mix, w, g, b, alpha, tm):
    T, D = x.shape
    tm = min(tm, T)
    rowspec = pl.BlockSpec((tm, D), lambda i: (i, 0))
    vec = pl.BlockSpec((1, D), lambda i: (0, 0))
    return pl.pallas_call(
        functools.partial(_outproj_kernel, alpha=alpha),
        grid=(T // tm,),
        in_specs=[rowspec, rowspec, pl.BlockSpec((D, D), lambda i: (0, 0)), vec, vec],
        out_specs=[rowspec, rowspec],
        out_shape=[jax.ShapeDtypeStruct((T, D), F32), jax.ShapeDtypeStruct((T, D), BF16)],
        compiler_params=_params("parallel"),
        name="outproj_ln",
    )(x, mix, w, g, b)


def _res_ln_kernel(x_ref, yt_ref, g_ref, b_ref, o_ref, ob_ref, *, alpha):
    r = _layer_norm(alpha * x_ref[...] + yt_ref[0].T, g_ref[...], b_ref[...])
    o_ref[...] = r
    ob_ref[...] = r.astype(BF16)


def _res_ln(x, yt, g, b, alpha):
    T, D = x.shape
    tm = yt.shape[2]
    rowspec = pl.BlockSpec((tm, D), lambda i: (i, 0))
    vec = pl.BlockSpec((1, D), lambda i: (0, 0))
    return pl.pallas_call(
        functools.partial(_res_ln_kernel, alpha=alpha),
        grid=(T // tm,),
        in_specs=[rowspec, pl.BlockSpec((1, D, tm), lambda i: (i, 0, 0)), vec, vec],
        out_specs=[rowspec, rowspec],
        out_shape=[jax.ShapeDtypeStruct((T, D), F32), jax.ShapeDtypeStruct((T, D), BF16)],
        compiler_params=_params("parallel"),
        name="res_ln",
    )(x, yt, g, b)


_CAND_NQ = (16, 8, 5, 4, 3, 2, 2, 2)
_CAND_ROWS = 16 + 7 * 8 + 8


def _extract_top(work, iota, n_iter, on_pick):
    big = float(work.shape[0])
    for p in range(n_iter):
        m = jnp.max(work, axis=0, keepdims=True)
        idx = jnp.min(jnp.where(work == m, iota, big), axis=0, keepdims=True)
        onehot = iota == idx
        work = jnp.where(onehot, NEG_INF, work)
        on_pick(p, m, onehot)


def _select_chunk(s1, s2, top_ref, cnt_ref, lanes, exact):
    K = P_TOPK
    n = s1.shape[1]
    iota = lax.broadcasted_iota(jnp.int32, (P_NKEYS, n), 0).astype(F32)
    ranks = []
    suspect = jnp.zeros((1, n), F32)
    for c, s in enumerate((s1, s2)):
        rank = jnp.full((P_NKEYS, n), float(K), F32)
        if exact:
            rank_holder = [rank]

            def on_pick(p, m, onehot, c=c, rank_holder=rank_holder):
                top_ref[c, p:p + 1, lanes] = m
                rank_holder[0] = jnp.where(onehot, float(p), rank_holder[0])

            _extract_top(s, iota, K, on_pick)
            rank = rank_holder[0]
        else:
            work = s
            for p in range(K):
                m = jnp.max(work, axis=0, keepdims=True)
                hit = work == m
                work = jnp.where(hit, NEG_INF, work)
                rank = jnp.where(hit, float(p), rank)
                top_ref[c, p:p + 1, lanes] = m
            n_ranked = jnp.sum(jnp.where(rank < float(K), 1.0, 0.0), axis=0, keepdims=True)
            suspect = suspect + jnp.abs(n_ranked - float(K))
        ranks.append(rank)

    a = top_ref[0, :, lanes]
    b = top_ref[1, :, lanes]
    ea = jnp.exp(a - a[0:1])
    eb = jnp.exp(b - b[0:1])
    q8 = lax.broadcasted_iota(jnp.int32, (8, n), 0)
    cand = [a[0:1] + b]
    wcand = [ea[0:1] * eb]
    for p in range(1, 8):
        ok = q8 < _CAND_NQ[p]
        cand.append(jnp.where(ok, a[p:p + 1] + b[0:8], NEG_INF))
        wcand.append(ea[p:p + 1] * eb[0:8])
    cand.append(a[8:16] + b[0:1])
    wcand.append(ea[8:16] * eb[0:1])
    cand = jnp.concatenate(cand, axis=0)
    wcand = jnp.concatenate(wcand, axis=0)
    if exact:
        iota_c = lax.broadcasted_iota(jnp.int32, (_CAND_ROWS, n), 0).astype(F32)
        sel_holder = [jnp.zeros((_CAND_ROWS, n), F32)]

        def on_pick2(p, m, onehot):
            sel_holder[0] = jnp.where(onehot, 1.0, sel_holder[0])

        _extract_top(cand, iota_c, K, on_pick2)
        sel = sel_holder[0]
    else:
        work = cand
        for p in range(K):
            m = jnp.max(work, axis=0, keepdims=True)
            work = jnp.where(work == m, NEG_INF, work)
        sel = jnp.where(work != cand, 1.0, 0.0)
        suspect = suspect + jnp.abs(jnp.sum(sel, axis=0, keepdims=True) - float(K))
    z = jnp.sum(sel * wcand, axis=0, keepdims=True)
    cnt_ref[0:1, lanes] = jnp.sum(sel[0:16], axis=0, keepdims=True)
    for p in range(1, 8):
        cnt_ref[p:p + 1, lanes] = jnp.sum(sel[8 + 8 * p:16 + 8 * p], axis=0, keepdims=True)
    cnt_ref[8:16, lanes] = sel[72:80]
    cnt = cnt_ref[:, lanes]
    cnt1 = jnp.zeros((P_NKEYS, n), F32)
    for p in range(K):
        cnt1 = jnp.where(ranks[0] == float(p), cnt[p:p + 1], cnt1)
    return ranks[1], cnt1, jnp.exp(s1 - a[0:1]), jnp.exp(s2 - b[0:1]) / z, suspect


def _peer_select_kernel(xb_ref, wq_ref, sk_ref, rank2_ref, cnt1_ref, e1_ref, e2_ref,
                        q_scr, s_scr, top_scr, cnt_scr):
    hg = pl.program_id(1)
    tm = xb_ref.shape[0]

    @pl.when(hg == 0)
    def _():
        q = jnp.dot(xb_ref[...], wq_ref[...], preferred_element_type=F32)
        for hc in range(2 * P_HEADS):
            q_scr[hc] = q[:, hc * LANES:(hc + 1) * LANES]

    for hh in range(SELECT_HEADS):
        for c in range(2):
            s_scr[hh, c] = _nt_dot(sk_ref[hh, c], q_scr[2 * (SELECT_HEADS * hg + hh) + c],
                                   precision=HIGHEST)

    def select_tile(exact):
        suspect = jnp.zeros((1, LANES), F32)
        for hh in range(SELECT_HEADS):
            for lc in range(tm // LANES):
                lanes = slice(lc * LANES, (lc + 1) * LANES)
                rank2, cnt1, e1, e2, sus = _select_chunk(
                    s_scr[hh, 0, :, lanes], s_scr[hh, 1, :, lanes], top_scr.at[hh],
                    cnt_scr.at[hh], lanes, exact)
                rank2_ref[hh, :, lanes] = rank2.astype(rank2_ref.dtype)
                cnt1_ref[hh, :, lanes] = cnt1
                e1_ref[hh, :, lanes] = e1
                e2_ref[hh, :, lanes] = e2.astype(e2_ref.dtype)
                suspect = suspect + sus
        return suspect

    suspect = select_tile(exact=False)

    @pl.when(jnp.max(suspect) > 0.0)
    def _():
        select_tile(exact=True)


def _peer_select(xb, wq, sk, tm):
    T, D = xb.shape
    tm = min(tm, T)
    out = jax.ShapeDtypeStruct((P_HEADS, P_NKEYS, T), F32)
    outb = jax.ShapeDtypeStruct((P_HEADS, P_NKEYS, T), BF16)
    ospec = pl.BlockSpec((SELECT_HEADS, P_NKEYS, tm), lambda i, h: (h, 0, i))
    return pl.pallas_call(
        _peer_select_kernel,
        grid=(T // tm, P_HEADS // SELECT_HEADS),
        in_specs=[pl.BlockSpec((tm, D), lambda i, h: (i, 0)),
                  pl.BlockSpec(wq.shape, lambda i, h: (0, 0)),
                  pl.BlockSpec((SELECT_HEADS, 2, P_NKEYS, LANES), lambda i, h: (h, 0, 0, 0))],
        out_specs=[ospec, ospec, ospec, ospec],
        out_shape=[outb, out, out, outb],
        scratch_shapes=[pltpu.VMEM((2 * P_HEADS, tm, LANES), F32),
                        pltpu.VMEM((SELECT_HEADS, 2, P_NKEYS, tm), F32),
                        pltpu.VMEM((SELECT_HEADS, 2, P_TOPK, tm), F32),
                        pltpu.VMEM((SELECT_HEADS, P_TOPK, tm), F32)],
        compiler_params=_params("parallel", "arbitrary"),
        name="peer_select",
    )(xb, wq, sk)


def _peer_gated_act(act_t, blk, rank2_ref, cnt1_ref, e1_ref, e2_ref, nsub):
    parts = []
    for ii in range(nsub):
        i = blk * nsub + ii
        w = None
        for h in range(P_HEADS):
            cnt = cnt1_ref[h, pl.ds(i, 1), :].astype(BF16)
            e1 = e1_ref[h, pl.ds(i, 1), :].astype(BF16)
            t = jnp.where(rank2_ref[h] < cnt, e2_ref[h], jnp.zeros((), BF16)) * e1
            w = t if w is None else w + t
        a = act_t[ii * P_NKEYS:(ii + 1) * P_NKEYS]
        gelu = 0.5 * a * (1.0 + lax.erf(a * (0.5 ** 0.5)))
        parts.append(w * gelu.astype(BF16))
    return jnp.concatenate(parts, axis=0) if nsub > 1 else parts[0]


def _peer_dense_kernel(xb_ref, rank2_ref, cnt1_ref, e1_ref, e2_ref, u_ref, *rest, nsub, pairs):
    vt_refs, (o_ref, pa_ref, pb_ref) = rest[:2 * pairs], rest[2 * pairs:]
    s = pl.program_id(1)
    ns = pl.num_programs(1) - 1
    sub = nsub * P_NKEYS
    sel = (rank2_ref, cnt1_ref, e1_ref, e2_ref)

    @pl.when(s == 0)
    def _():
        o_ref[0] = jnp.zeros(o_ref.shape[1:], F32)
        pa_ref[...] = jnp.zeros_like(pa_ref)

    @pl.when(s < ns)
    def _():
        xb = xb_ref[...]
        for r in range(pairs):
            blk = 2 * (pairs * s + r)
            act0 = _nt_dot(u_ref[2 * r * sub:(2 * r + 1) * sub, :], xb)
            act1 = _nt_dot(u_ref[(2 * r + 1) * sub:(2 * r + 2) * sub, :], xb)
            o_ref[0] += jnp.dot(vt_refs[2 * r][0], pa_ref[...], preferred_element_type=F32)
            pb_ref[...] = _peer_gated_act(act0, blk, *sel, nsub)
            o_ref[0] += jnp.dot(vt_refs[2 * r + 1][0], pb_ref[...], preferred_element_type=F32)
            pa_ref[...] = _peer_gated_act(act1, blk + 1, *sel, nsub)

    @pl.when(s == ns)
    def _():
        o_ref[0] += jnp.dot(vt_refs[0][0], pa_ref[...], preferred_element_type=F32)


def _peer_dense(xb, rank2, cnt1, e1, e2, u_b, vt_b, tm, pairs=2):
    T, D = xb.shape
    tm = min(tm, T)
    nblk, _, be = vt_b.shape
    nsub = be // P_NKEYS
    per_step = 2 * pairs
    ns = nblk // per_step
    sel = pl.BlockSpec((P_HEADS, P_NKEYS, tm), lambda i, s: (0, 0, i))

    def vt_spec(k):
        return pl.BlockSpec((1, D, be),
                            lambda i, s: (jnp.clip(per_step * s - 1 + k, 0, nblk - 1), 0, 0))

    return pl.pallas_call(
        functools.partial(_peer_dense_kernel, nsub=nsub, pairs=pairs),
        grid=(T // tm, ns + 1),
        in_specs=[pl.BlockSpec((tm, D), lambda i, s: (i, 0)), sel, sel, sel, sel,
                  pl.BlockSpec((per_step * be, D), lambda i, s: (jnp.minimum(s, ns - 1), 0))]
                 + [vt_spec(k) for k in range(per_step)],
        out_specs=pl.BlockSpec((1, D, tm), lambda i, s: (i, 0, 0)),
        out_shape=jax.ShapeDtypeStruct((T // tm, D, tm), F32),
        scratch_shapes=[pltpu.VMEM((be, tm), BF16), pltpu.VMEM((be, tm), BF16)],
        compiler_params=_params("parallel", "arbitrary"),
        name="peer_dense",
    )(xb, rank2, cnt1, e1, e2, u_b, *([vt_b] * per_step))


def _pad_row(vals_at, width=LANES):
    row = jnp.zeros((1, width), F32)
    for off, v in vals_at:
        row = row.at[0, off:off + v.shape[0]].set(v.astype(F32))
    return row


def _rope_tables(S):
    half = A_HEADDIM // 2
    freqs = ROPE_THETA ** (-jnp.arange(half, dtype=F32) / half)
    ang = jnp.arange(S, dtype=F32)[:, None] * freqs[None, :]
    reps = LANES // half
    return jnp.tile(jnp.cos(ang), (1, reps)), jnp.tile(jnp.sin(ang), (1, reps))


def _head_expand_matrix():
    e = np.zeros((S_GROUPS, LANES, S_HPG * S_HEADDIM), np.float32)
    for g in range(S_GROUPS):
        for hh in range(S_HPG):
            e[g, SM_DT + g * S_HPG + hh, hh * S_HEADDIM:(hh + 1) * S_HEADDIM] = 1.0
    return jnp.asarray(e)


def _layer(x, xb, p, consts, B, S, alpha):
    T, D = x.shape
    w_in = p["w_in"]
    o_m = 2 * M_HEADS * M_DQK + 2 * M_HEADS * M_DV
    o_s = o_m + 2 * M_HEADS
    o_dt = o_s + S_DINNER + S_CONV_CH
    o_a = o_dt + S_HEADS
    o_g = o_a + (A_QHEADS + 2 * A_KVHEADS) * A_HEADDIM
    w_m = w_in[:, :o_m].astype(BF16)
    w_s = w_in[:, o_s:o_dt].astype(BF16)
    w_a = w_in[:, o_a:o_g].astype(BF16)
    w_g = w_in[:, o_g:].astype(BF16)
    w_sm = jnp.concatenate([w_in[:, o_m:o_s], w_in[:, o_dt:o_a],
                            jnp.zeros((D, LANES - 2 * M_HEADS - S_HEADS), F32)], axis=1).astype(BF16)

    mp = _matmul(xb, w_m, 1024, 1536)
    sp = _matmul(xb, w_s, 1024, 1280)
    ap = _matmul(xb, w_a, 1024, 1536)
    gp = _matmul(xb, w_g, 1024, 2048)
    sm = _matmul(xb, w_sm, 1024, LANES)

    brow = _pad_row([(SM_I, p["mlstm_gate_b"][0]), (SM_F, p["mlstm_gate_b"][1]),
                     (SM_DT, p["ssm_dt_bias"])])
    arow = _pad_row([(SM_DT, p["ssm_a_log"])])
    drow = jnp.repeat(p["ssm_d"].astype(F32), S_HEADDIM)[None, :]
    y_m = _mlstm(mp, sm, brow, p["mlstm_norm_w"][None, :], B, S)
    y_s = _ssd(sp, sm, brow, arow, drow, p["ssm_conv_w"][:, 0, :], p["ssm_conv_b"][None, :],
               p["ssm_norm_w"][None, :], consts["e_mat"], B, S)
    y_a = _swa(ap, consts["cos"], consts["sin"], _pad_row([(0, p["swa_sinks"])]), B, S)

    mix = _merge(y_m, y_s, y_a, gp, p["merge_gate_b"].reshape(1, 3 * D),
                 p["w_branch"].astype(BF16), 1024, 512)
    x1, x1b = _outproj(x, mix, p["w_out"].astype(BF16), p["ln1_g"][None, :], p["ln1_b"][None, :],
                       alpha, 512)

    rank2, cnt1, e1, e2 = _peer_select(x1b, p["peer_wq"].astype(BF16), p["peer_subkeys"], 512)
    be = 256
    vt_blocks = jnp.swapaxes(p["peer_v"].astype(BF16).reshape(P_EXPERTS // be, be, D), 1, 2)
    peer = _peer_dense(x1b, rank2, cnt1, e1, e2, p["peer_u"].astype(BF16), vt_blocks, 512)
    return _res_ln(x1, peer, p["ln2_g"][None, :], p["ln2_b"][None, :], alpha)


def _forward(x, params, depth):
    B, S, D = x.shape
    alpha = (2.0 * depth) ** 0.25
    cos_t, sin_t = _rope_tables(S)
    consts = {"cos": cos_t, "sin": sin_t, "e_mat": _head_expand_matrix()}
    xf = x.reshape(B * S, D)

    def body(carry, p):
        xc, xcb = carry
        return _layer(xc, xcb, p, consts, B, S, alpha), None

    (xf, _), _ = lax.scan(body, (xf, xf.astype(BF16)), params)
    return xf.reshape(B, S, D)


def kernel(x, w_in, mlstm_gate_b, mlstm_norm_w, ssm_conv_w, ssm_conv_b, ssm_dt_bias, ssm_a_log,
           ssm_d, ssm_norm_w, swa_sinks, merge_gate_b, w_branch, w_out, ln1_g, ln1_b,
           peer_wq, peer_subkeys, peer_u, peer_v, ln2_g, ln2_b):
    params = dict(w_in=w_in, mlstm_gate_b=mlstm_gate_b, mlstm_norm_w=mlstm_norm_w,
                  ssm_conv_w=ssm_conv_w, ssm_conv_b=ssm_conv_b, ssm_dt_bias=ssm_dt_bias,
                  ssm_a_log=ssm_a_log, ssm_d=ssm_d, ssm_norm_w=ssm_norm_w, swa_sinks=swa_sinks,
                  merge_gate_b=merge_gate_b, w_branch=w_branch, w_out=w_out, ln1_g=ln1_g,
                  ln1_b=ln1_b, peer_wq=peer_wq, peer_subkeys=peer_subkeys, peer_u=peer_u,
                  peer_v=peer_v, ln2_g=ln2_g, ln2_b=ln2_b)
    return _forward(x, params, w_in.shape[0])
```

```python
import functools
import math

import jax
import jax.numpy as jnp
import numpy as np
from jax import lax
from jax.experimental import pallas as pl
from jax.experimental.pallas import tpu as pltpu

F32 = jnp.float32
BF16 = jnp.bfloat16
HIGHEST = lax.Precision.HIGHEST

LN_EPS = 1e-5
NORM_EPS = 1e-6

LANES = 128
CHUNK = 128
STEP_CHUNKS = 4
VMEM_LIMIT = 56 * 1024 * 1024

M_HEADS, M_DQK, M_DV = 4, 128, 256
S_HEADS, S_GROUPS, S_HPG, S_HEADDIM, S_DSTATE, S_CONV = 16, 2, 8, 64, 128, 4
S_DINNER = 1024
S_CONV_CH = S_DINNER + 2 * S_GROUPS * S_DSTATE
A_QHEADS, A_KVHEADS, A_HEADDIM, A_REP = 16, 4, 64, 4
ROPE_THETA = 10000.0
P_HEADS, P_NKEYS, P_TOPK = 8, 128, 16
P_EXPERTS = P_NKEYS * P_NKEYS
BRANCH_WIDTH = 1024

SM_I, SM_F, SM_DT = 0, M_HEADS, 2 * M_HEADS

NEG_INF = float("-inf")


def _params(*sem, flags=None):
    return pltpu.CompilerParams(dimension_semantics=sem, vmem_limit_bytes=VMEM_LIMIT, flags=flags)


def _nt_dot(a, b, **kw):
    return lax.dot_general(a, b, (((1,), (1,)), ((), ())), preferred_element_type=F32, **kw)


def _softplus(x):
    return jnp.maximum(x, 0.0) + jnp.log(1.0 + jnp.exp(-jnp.abs(x)))


def _log_sigmoid(x):
    return jnp.minimum(x, 0.0) - jnp.log(1.0 + jnp.exp(-jnp.abs(x)))


def _sigmoid(x):
    return 1.0 / (1.0 + jnp.exp(-x))


def _silu(x):
    return x * _sigmoid(x)


def _layer_norm(y, g, b):
    mu = jnp.mean(y, -1, keepdims=True)
    d = y - mu
    var = jnp.mean(d * d, -1, keepdims=True)
    return d * lax.rsqrt(var + LN_EPS) * g + b


def _mm_kernel(x_ref, w_ref, o_ref):
    o_ref[...] = jnp.dot(x_ref[...], w_ref[...], preferred_element_type=F32).astype(o_ref.dtype)


def _matmul(x, w, tm, tn, out_dtype=F32):
    T, K = x.shape
    N = w.shape[1]
    tm, tn = min(tm, T), min(tn, N)
    return pl.pallas_call(
        _mm_kernel,
        grid=(T // tm, N // tn),
        in_specs=[pl.BlockSpec((tm, K), lambda i, j: (i, 0)),
                  pl.BlockSpec((K, tn), lambda i, j: (0, j))],
        out_specs=pl.BlockSpec((tm, tn), lambda i, j: (i, j)),
        out_shape=jax.ShapeDtypeStruct((T, N), out_dtype),
        compiler_params=_params("parallel", "arbitrary"),
        name="proj_matmul",
    )(x, w)


def _mlstm_kernel(mp_ref, sm_ref, brow_ref, nw_ref, o_ref, c_ref, n_ref, m_ref):
    L = CHUNK

    @pl.when(pl.program_id(1) == 0)
    def _():
        c_ref[...] = jnp.zeros_like(c_ref)
        n_ref[...] = jnp.zeros_like(n_ref)
        m_ref[...] = jnp.zeros_like(m_ref)

    row = lax.broadcasted_iota(jnp.int32, (L, L), 0)
    col = lax.broadcasted_iota(jnp.int32, (L, L), 1)
    tril = row >= col
    trilf = tril.astype(F32)
    scale = M_DQK ** -0.5
    for cc, h in [(cc, h) for cc in range(STEP_CHUNKS) for h in range(M_HEADS)]:
        rows = slice(cc * L, (cc + 1) * L)
        gates = sm_ref[rows, :] + brow_ref[...]
        q = mp_ref[rows, h * M_DQK:(h + 1) * M_DQK]
        k = mp_ref[rows, 512 + h * M_DQK:512 + (h + 1) * M_DQK] * scale
        v = mp_ref[rows, 1024 + h * M_DV:1024 + (h + 1) * M_DV].astype(BF16)
        og = mp_ref[rows, 2048 + h * M_DV:2048 + (h + 1) * M_DV]
        li_col = gates[:, SM_I + h:SM_I + h + 1]
        lf_col = _log_sigmoid(gates[:, SM_F + h:SM_F + h + 1])
        bc = jnp.dot(trilf, jnp.broadcast_to(lf_col, (L, L)), precision=HIGHEST,
                     preferred_element_type=F32)
        br = bc.T
        lir = jnp.broadcast_to(li_col, (L, L)).T
        b_col = bc[:, 0:1]
        m_prev = m_ref[h, 0:1, 0:1]
        log_d = jnp.where(tril, bc - br + lir, NEG_INF)
        m_t = jnp.maximum(jnp.max(log_d, axis=1, keepdims=True), b_col + m_prev)
        w_intra = jnp.exp(log_d - m_t)
        w_inter = jnp.exp(b_col + m_prev - m_t)
        qb = q.astype(BF16)
        s = _nt_dot(qb, k.astype(BF16)) * w_intra
        c_old = c_ref[h]
        n_old = n_ref[h, 0:1, :]
        num = (jnp.dot(s.astype(BF16), v, preferred_element_type=F32)
               + w_inter * jnp.dot(qb, c_old.astype(BF16), preferred_element_type=F32))
        den = (jnp.sum(s, axis=1, keepdims=True)
               + w_inter * jnp.sum(q * n_old, axis=1, keepdims=True))
        hh = num / jnp.maximum(jnp.abs(den), jnp.exp(-m_t))

        g = bc[L - 1:L, 0:1]
        lws = g - b_col + li_col
        m_new = jnp.maximum(g + m_prev, jnp.max(lws, axis=0, keepdims=True))
        a_prev = jnp.exp(g + m_prev - m_new)
        kw = k * jnp.exp(lws - m_new)
        c_ref[h] = a_prev * c_old + jnp.dot(kw.T.astype(BF16), v, preferred_element_type=F32)
        n_ref[h, 0:1, :] = a_prev * n_old + jnp.sum(kw, axis=0, keepdims=True)
        m_ref[h] = jnp.broadcast_to(m_new, m_ref.shape[1:])

        mu = jnp.mean(hh, -1, keepdims=True)
        d = hh - mu
        var = jnp.mean(d * d, -1, keepdims=True)
        hn = d * lax.rsqrt(var + NORM_EPS) * nw_ref[:, h * M_DV:(h + 1) * M_DV]
        o_ref[rows, h * M_DV:(h + 1) * M_DV] = (_sigmoid(og) * hn).astype(o_ref.dtype)


def _mlstm(mp, sm, brow, nw, B, S):
    R = STEP_CHUNKS * CHUNK
    nc = S // R
    W = mp.shape[1]
    return pl.pallas_call(
        _mlstm_kernel,
        grid=(B, nc),
        in_specs=[pl.BlockSpec((R, W), lambda b, c: (b * nc + c, 0)),
                  pl.BlockSpec((R, LANES), lambda b, c: (b * nc + c, 0)),
                  pl.BlockSpec((1, LANES), lambda b, c: (0, 0)),
                  pl.BlockSpec((1, BRANCH_WIDTH), lambda b, c: (0, 0))],
        out_specs=pl.BlockSpec((R, BRANCH_WIDTH), lambda b, c: (b * nc + c, 0)),
        out_shape=jax.ShapeDtypeStruct((B * S, BRANCH_WIDTH), BF16),
        scratch_shapes=[pltpu.VMEM((M_HEADS, M_DQK, M_DV), F32),
                        pltpu.VMEM((M_HEADS, 8, M_DQK), F32),
                        pltpu.VMEM((M_HEADS, 8, LANES), F32)],
        compiler_params=_params("parallel", "arbitrary"),
        name="mlstm",
    )(mp, sm, brow, nw)


def _ssd_kernel(cur_ref, prev_ref, sm_ref, brow_ref, arow_ref, drow_ref, cw_ref, cb_ref, nw_ref,
                e_ref, o_ref, st_ref):
    L = CHUNK
    c = pl.program_id(1)

    @pl.when(c == 0)
    def _():
        st_ref[...] = jnp.zeros_like(st_ref)

    ridx = lax.broadcasted_iota(jnp.int32, (L, S_CONV_CH), 0)
    row = lax.broadcasted_iota(jnp.int32, (L, L), 0)
    col = lax.broadcasted_iota(jnp.int32, (L, L), 1)
    tril = row >= col
    gw = S_HPG * S_HEADDIM
    for cc in range(STEP_CHUNKS):
        rows = slice(cc * L, (cc + 1) * L)
        u = cur_ref[rows, S_DINNER:]
        if cc == 0:
            up = prev_ref[:, S_DINNER:] * (c > 0).astype(F32)
        else:
            up = cur_ref[(cc - 1) * L:cc * L, S_DINNER:]
        conv = cb_ref[...] + cw_ref[S_CONV - 1:S_CONV, :] * u
        for kk in range(S_CONV - 1):
            sh = S_CONV - 1 - kk
            shifted = jnp.where(ridx < sh, pltpu.roll(up, sh, 0), pltpu.roll(u, sh, 0))
            conv = conv + cw_ref[kk:kk + 1, :] * shifted
        xbc = _silu(conv)

        dt = _softplus(sm_ref[rows, :] + brow_ref[...])
        d_a = dt * (-jnp.exp(arow_ref[...]))
        acum = jnp.dot(tril.astype(F32), d_a, precision=HIGHEST, preferred_element_type=F32)

        for g in range(S_GROUPS):
            xg = xbc[:, g * gw:(g + 1) * gw]
            bg = xbc[:, S_DINNER + g * S_DSTATE:S_DINNER + (g + 1) * S_DSTATE]
            co = S_DINNER + (S_GROUPS + g) * S_DSTATE
            cgb = xbc[:, co:co + S_DSTATE].astype(BF16)
            cb = _nt_dot(cgb, bg.astype(BF16))
            e_g = e_ref[g]
            dt_e = jnp.dot(dt, e_g, precision=HIGHEST, preferred_element_type=F32)
            acum_e = jnp.dot(acum, e_g, precision=HIGHEST, preferred_element_type=F32)
            xc = xg * dt_e
            xcb = xc.astype(BF16)
            last = acum_e[L - 1:L, :]
            ys = []
            for hh in range(S_HPG):
                lane = SM_DT + g * S_HPG + hh
                ac = jnp.broadcast_to(acum[:, lane:lane + 1], (L, L))
                dec = jnp.exp(jnp.where(tril, ac - ac.T, NEG_INF))
                ys.append(jnp.dot((cb * dec).astype(BF16),
                                  xcb[:, hh * S_HEADDIM:(hh + 1) * S_HEADDIM],
                                  preferred_element_type=F32))
            y = jnp.concatenate(ys, axis=1)
            st = st_ref[g]
            y = y + jnp.dot(cgb, st.astype(BF16), preferred_element_type=F32) * jnp.exp(acum_e)
            upd = jnp.dot(bg.T.astype(BF16), (jnp.exp(last - acum_e) * xc).astype(BF16),
                          preferred_element_type=F32)
            st_ref[g] = jnp.exp(last) * st + upd
            y = y + drow_ref[:, g * gw:(g + 1) * gw] * xg
            y = y * _silu(cur_ref[rows, g * gw:(g + 1) * gw])
            y = y * lax.rsqrt(jnp.mean(y * y, -1, keepdims=True) + NORM_EPS)
            o_ref[rows, g * gw:(g + 1) * gw] = (
                y * nw_ref[:, g * gw:(g + 1) * gw]).astype(o_ref.dtype)


def _ssd(sp, sm, brow, arow, drow, cw, cb, nw, e_mat, B, S):
    R = STEP_CHUNKS * CHUNK
    nc = S // R
    W = sp.shape[1]
    const = lambda b, c: (0, 0)
    return pl.pallas_call(
        _ssd_kernel,
        grid=(B, nc),
        in_specs=[pl.BlockSpec((R, W), lambda b, c: (b * nc + c, 0)),
                  pl.BlockSpec((CHUNK, W),
                               lambda b, c: (STEP_CHUNKS * (b * nc + c) - jnp.where(c > 0, 1, 0), 0)),
                  pl.BlockSpec((R, LANES), lambda b, c: (b * nc + c, 0)),
                  pl.BlockSpec((1, LANES), const),
                  pl.BlockSpec((1, LANES), const),
                  pl.BlockSpec((1, S_DINNER), const),
                  pl.BlockSpec((S_CONV, S_CONV_CH), const),
                  pl.BlockSpec((1, S_CONV_CH), const),
                  pl.BlockSpec((1, S_DINNER), const),
                  pl.BlockSpec((S_GROUPS, LANES, S_HPG * S_HEADDIM), lambda b, c: (0, 0, 0))],
        out_specs=pl.BlockSpec((R, S_DINNER), lambda b, c: (b * nc + c, 0)),
        out_shape=jax.ShapeDtypeStruct((B * S, S_DINNER), BF16),
        scratch_shapes=[pltpu.VMEM((S_GROUPS, S_DSTATE, S_HPG * S_HEADDIM), F32)],
        compiler_params=_params("parallel", "arbitrary"),
        name="ssd",
    )(sp, sp, sm, brow, arow, drow, cw, cb, nw, e_mat)


def _rope(x, cos, sin):
    lane = lax.broadcasted_iota(jnp.int32, x.shape, 1)
    first_half = (lane % A_HEADDIM) < (A_HEADDIM // 2)
    rot = jnp.where(first_half, -pltpu.roll(x, LANES - A_HEADDIM // 2, 1),
                    pltpu.roll(x, A_HEADDIM // 2, 1))
    return x * cos + rot * sin


def _swa_kernel(cur_ref, prev_ref, cos_ref, sin_ref, cosp_ref, sinp_ref, sink_ref, o_ref):
    L = CHUNK
    n = pl.program_id(1)
    qo, ko, vo = 0, A_QHEADS * A_HEADDIM, (A_QHEADS + A_KVHEADS) * A_HEADDIM
    vw = A_KVHEADS * A_HEADDIM

    def roped_keys(ref, rows, cos, sin):
        return jnp.concatenate([_rope(ref[rows, ko + j * LANES:ko + (j + 1) * LANES], cos, sin)
                                for j in range(2)], axis=1).astype(BF16)

    everything = slice(None)
    keys = [roped_keys(prev_ref, everything, cosp_ref[...], sinp_ref[...])]
    vals = [prev_ref[:, vo:vo + vw]]
    for cc in range(STEP_CHUNKS):
        rows = slice(cc * L, (cc + 1) * L)
        keys.append(roped_keys(cur_ref, rows, cos_ref[rows, :], sin_ref[rows, :]))
        vals.append(cur_ref[rows, vo:vo + vw])
    krow = lax.broadcasted_iota(jnp.int32, (2 * L, L), 0)
    qcol = lax.broadcasted_iota(jnp.int32, (2 * L, L), 1)
    band = jnp.logical_and(krow > qcol, krow <= qcol + L)
    scale = A_HEADDIM ** -0.5
    for cc in range(STEP_CHUNKS):
        rows = slice(cc * L, (cc + 1) * L)
        cos, sin = cos_ref[rows, :], sin_ref[rows, :]
        kk = jnp.concatenate([keys[cc], keys[cc + 1]], axis=0)
        vv_t = jnp.concatenate([vals[cc], vals[cc + 1]], axis=0).T
        valid = jnp.logical_and(band, krow >= jnp.where(n > 0, 0, L)) if cc == 0 else band
        for j in range(A_QHEADS // 2):
            qpair = _rope(cur_ref[rows, qo + j * LANES:qo + (j + 1) * LANES], cos, sin).astype(BF16)
            outs = []
            for t in range(2):
                hq = 2 * j + t
                g = hq // A_REP
                sl = slice(g * A_HEADDIM, (g + 1) * A_HEADDIM)
                qh = qpair[:, t * A_HEADDIM:(t + 1) * A_HEADDIM]
                s_t = jnp.where(valid, _nt_dot(kk[:, sl], qh) * scale, NEG_INF)
                sink = sink_ref[:, hq:hq + 1]
                m = jnp.maximum(jnp.max(s_t, axis=0, keepdims=True), sink)
                p = jnp.exp(s_t - m)
                den = jnp.sum(p, axis=0, keepdims=True) + jnp.exp(sink - m)
                o_t = jnp.dot(vv_t[sl, :].astype(BF16), p.astype(BF16),
                              preferred_element_type=F32)
                outs.append(o_t / den)
            o_ref[rows, j * LANES:(j + 1) * LANES] = (
                jnp.concatenate(outs, axis=0).T.astype(o_ref.dtype))


def _swa(ap, cos_t, sin_t, sink_row, B, S):
    R = STEP_CHUNKS * CHUNK
    nb = S // R
    W = ap.shape[1]
    cur = lambda b, n: (b * nb + n, 0)
    prev = lambda b, n: (STEP_CHUNKS * (b * nb + n) - jnp.where(n > 0, 1, 0), 0)
    tab = lambda b, n: (n, 0)
    tabp = lambda b, n: (jnp.maximum(STEP_CHUNKS * n - 1, 0), 0)
    return pl.pallas_call(
        _swa_kernel,
        grid=(B, nb),
        in_specs=[pl.BlockSpec((R, W), cur), pl.BlockSpec((CHUNK, W), prev),
                  pl.BlockSpec((R, LANES), tab), pl.BlockSpec((R, LANES), tab),
                  pl.BlockSpec((CHUNK, LANES), tabp), pl.BlockSpec((CHUNK, LANES), tabp),
                  pl.BlockSpec((1, LANES), lambda b, n: (0, 0))],
        out_specs=pl.BlockSpec((R, BRANCH_WIDTH), cur),
        out_shape=jax.ShapeDtypeStruct((B * S, BRANCH_WIDTH), BF16),
        compiler_params=_params("parallel", "parallel"),
        name="swa",
    )(ap, ap, cos_t, sin_t, cos_t, sin_t, sink_row)


def _merge_kernel(xb_ref, ym_ref, ys_ref, ya_ref, g0_ref, g1_ref, g2_ref, b0_ref, b1_ref, b2_ref,
                  w0_ref, w1_ref, w2_ref, o_ref):
    xb = xb_ref[...]
    acc = None
    for y_ref, g_ref, b_ref, w_ref in ((ym_ref, g0_ref, b0_ref, w0_ref),
                                       (ys_ref, g1_ref, b1_ref, w1_ref),
                                       (ya_ref, g2_ref, b2_ref, w2_ref)):
        gate = _sigmoid(jnp.dot(xb, g_ref[...], preferred_element_type=F32) + b_ref[...])
        t = gate * jnp.dot(y_ref[...], w_ref[0], preferred_element_type=F32)
        acc = t if acc is None else acc + t
    o_ref[...] = acc.astype(o_ref.dtype)


def _merge(xb, ym, ys, ya, wg, gb, wb, tm, tn):
    T, K = xb.shape
    D = wb.shape[2]
    tm, tn = min(tm, T), min(tn, D)
    nj = D // tn
    yspec = pl.BlockSpec((tm, BRANCH_WIDTH), lambda i, j: (i, 0))
    gspec = lambda k: pl.BlockSpec((K, tn), lambda i, j: (0, k * nj + j))
    bspec = lambda k: pl.BlockSpec((1, tn), lambda i, j: (0, k * nj + j))
    wspec = lambda k: pl.BlockSpec((1, BRANCH_WIDTH, tn), lambda i, j: (k, 0, j))
    return pl.pallas_call(
        _merge_kernel,
        grid=(T // tm, nj),
        in_specs=[pl.BlockSpec((tm, K), lambda i, j: (i, 0)), yspec, yspec, yspec,
                  gspec(0), gspec(1), gspec(2), bspec(0), bspec(1), bspec(2),
                  wspec(0), wspec(1), wspec(2)],
        out_specs=pl.BlockSpec((tm, tn), lambda i, j: (i, j)),
        out_shape=jax.ShapeDtypeStruct((T, D), BF16),
        compiler_params=_params("parallel", "arbitrary"),
        name="merge",
    )(xb, ym, ys, ya, wg, wg, wg, gb, gb, gb, wb, wb, wb)


def _outproj_kernel(x_ref, mix_ref, w_ref, g_ref, b_ref, o_ref, ob_ref, *, alpha):
    y = alpha * x_ref[...] + jnp.dot(mix_ref[...], w_ref[...], preferred_element_type=F32)
    r = _layer_norm(y, g_ref[...], b_ref[...])
    o_ref[...] = r
    ob_ref[...] = r.astype(BF16)


def _outproj(x, mix, w, g, b, alpha, tm):
    T, D = x.shape
    tm = min(tm, T)
    rowspec = pl.BlockSpec((tm, D), lambda i: (i, 0))
    vec = pl.BlockSpec((1, D), lambda i: (0, 0))
    return pl.pallas_call(
        functools.partial(_outproj_kernel, alpha=alpha),
        grid=(T // tm,),
        in_specs=[rowspec, rowspec, pl.BlockSpec((D, D), lambda i: (0, 0)), vec, vec],
        out_specs=[rowspec, rowspec],
        out_shape=[jax.ShapeDtypeStruct((T, D), F32), jax.ShapeDtypeStruct((T, D), BF16)],
        compiler_params=_params("parallel"),
        name="outproj_ln",
    )(x, mix, w, g, b)


def _res_ln_kernel(x_ref, yt_ref, g_ref, b_ref, o_ref, ob_ref, *, alpha):
    r = _layer_norm(alpha * x_ref[...] + yt_ref[0].T, g_ref[...], b_ref[...])
    o_ref[...] = r
    ob_ref[...] = r.astype(BF16)


def _res_ln(x, yt, g, b, alpha):
    T, D = x.shape
    tm = yt.shape[2]
    rowspec = pl.BlockSpec((tm, D), lambda i: (i, 0))
    vec = pl.BlockSpec((1, D), lambda i: (0, 0))
    return pl.pallas_call(
        functools.partial(_res_ln_kernel, alpha=alpha),
        grid=(T // tm,),
        in_specs=[rowspec, pl.BlockSpec((1, D, tm), lambda i: (i, 0, 0)), vec, vec],
        out_specs=[rowspec, rowspec],
        out_shape=[jax.ShapeDtypeStruct((T, D), F32), jax.ShapeDtypeStruct((T, D), BF16)],
        compiler_params=_params("parallel"),
        name="res_ln",
    )(x, yt, g, b)


_CAND_NQ = (16, 8, 5, 4, 3, 2, 2, 2)
_CAND_ROWS = 16 + 7 * 8 + 8


def _extract_top(work, iota, n_iter, on_pick):
    big = float(work.shape[0])
    for p in range(n_iter):
        m = jnp.max(work, axis=0, keepdims=True)
        idx = jnp.min(jnp.where(work == m, iota, big), axis=0, keepdims=True)
        onehot = iota == idx
        work = jnp.where(onehot, NEG_INF, work)
        on_pick(p, m, onehot)


def _select_chunk(s1, s2, top_ref, cnt_ref, lanes, exact):
    K = P_TOPK
    n = s1.shape[1]
    iota = lax.broadcasted_iota(jnp.int32, (P_NKEYS, n), 0).astype(F32)
    ranks = []
    suspect = jnp.zeros((1, n), F32)
    for c, s in enumerate((s1, s2)):
        rank = jnp.full((P_NKEYS, n), float(K), F32)
        if exact:
            rank_holder = [rank]

            def on_pick(p, m, onehot, c=c, rank_holder=rank_holder):
                top_ref[c, p:p + 1, lanes] = m
                rank_holder[0] = jnp.where(onehot, float(p), rank_holder[0])

            _extract_top(s, iota, K, on_pick)
            rank = rank_holder[0]
        else:
            work = s
            for p in range(K):
                m = jnp.max(work, axis=0, keepdims=True)
                hit = work == m
                work = jnp.where(hit, NEG_INF, work)
                rank = jnp.where(hit, float(p), rank)
                top_ref[c, p:p + 1, lanes] = m
            n_ranked = jnp.sum(jnp.where(rank < float(K), 1.0, 0.0), axis=0, keepdims=True)
            suspect = suspect + jnp.abs(n_ranked - float(K))
        ranks.append(rank)

    a = top_ref[0, :, lanes]
    b = top_ref[1, :, lanes]
    ea = jnp.exp(a - a[0:1])
    eb = jnp.exp(b - b[0:1])
    q8 = lax.broadcasted_iota(jnp.int32, (8, n), 0)
    cand = [a[0:1] + b]
    wcand = [ea[0:1] * eb]
    for p in range(1, 8):
        ok = q8 < _CAND_NQ[p]
        cand.append(jnp.where(ok, a[p:p + 1] + b[0:8], NEG_INF))
        wcand.append(ea[p:p + 1] * eb[0:8])
    cand.append(a[8:16] + b[0:1])
    wcand.append(ea[8:16] * eb[0:1])
    cand = jnp.concatenate(cand, axis=0)
    wcand = jnp.concatenate(wcand, axis=0)
    if exact:
        iota_c = lax.broadcasted_iota(jnp.int32, (_CAND_ROWS, n), 0).astype(F32)
        sel_holder = [jnp.zeros((_CAND_ROWS, n), F32)]

        def on_pick2(p, m, onehot):
            sel_holder[0] = jnp.where(onehot, 1.0, sel_holder[0])

        _extract_top(cand, iota_c, K, on_pick2)
        sel = sel_holder[0]
    else:
        work = cand
        for p in range(K):
            m = jnp.max(work, axis=0, keepdims=True)
            work = jnp.where(work == m, NEG_INF, work)
        sel = jnp.where(work != cand, 1.0, 0.0)
        suspect = suspect + jnp.abs(jnp.sum(sel, axis=0, keepdims=True) - float(K))
    z = jnp.sum(sel * wcand, axis=0, keepdims=True)
    cnt_ref[0:1, lanes] = jnp.sum(sel[0:16], axis=0, keepdims=True)
    for p in range(1, 8):
        cnt_ref[p:p + 1, lanes] = jnp.sum(sel[8 + 8 * p:16 + 8 * p], axis=0, keepdims=True)
    cnt_ref[8:16, lanes] = sel[72:80]
    cnt = cnt_ref[:, lanes]
    cnt1 = jnp.zeros((P_NKEYS, n), F32)
    for p in range(K):
        cnt1 = jnp.where(ranks[0] == float(p), cnt[p:p + 1], cnt1)
    return ranks[1], cnt1, jnp.exp(s1 - a[0:1]), jnp.exp(s2 - b[0:1]) / z, suspect


def _peer_select_kernel(xb_ref, wq_ref, sk_ref, rank2_ref, cnt1_ref, e1_ref, e2_ref,
                        q_scr, s_scr, top_scr, cnt_scr):
    h = pl.program_id(1)
    tm = xb_ref.shape[0]

    @pl.when(h == 0)
    def _():
        q = jnp.dot(xb_ref[...], wq_ref[...], preferred_element_type=F32)
        for hc in range(2 * P_HEADS):
            q_scr[hc] = q[:, hc * LANES:(hc + 1) * LANES]

    for c in range(2):
        s_scr[c] = _nt_dot(sk_ref[0, c], q_scr[2 * h + c], precision=HIGHEST)

    def select_tile(exact):
        suspect = jnp.zeros((1, LANES), F32)
        for lc in range(tm // LANES):
            lanes = slice(lc * LANES, (lc + 1) * LANES)
            rank2, cnt1, e1, e2, sus = _select_chunk(s_scr[0, :, lanes], s_scr[1, :, lanes],
                                                     top_scr, cnt_scr, lanes, exact)
            rank2_ref[0, :, lanes] = rank2.astype(rank2_ref.dtype)
            cnt1_ref[0, :, lanes] = cnt1
            e1_ref[0, :, lanes] = e1
            e2_ref[0, :, lanes] = e2.astype(e2_ref.dtype)
            suspect = suspect + sus
        return suspect

    suspect = select_tile(exact=False)

    @pl.when(jnp.max(suspect) > 0.0)
    def _():
        select_tile(exact=True)


def _peer_select(xb, wq, sk, tm):
    T, D = xb.shape
    tm = min(tm, T)
    out = jax.ShapeDtypeStruct((P_HEADS, P_NKEYS, T), F32)
    outb = jax.ShapeDtypeStruct((P_HEADS, P_NKEYS, T), BF16)
    ospec = pl.BlockSpec((1, P_NKEYS, tm), lambda i, h: (h, 0, i))
    return pl.pallas_call(
        _peer_select_kernel,
        grid=(T // tm, P_HEADS),
        in_specs=[pl.BlockSpec((tm, D), lambda i, h: (i, 0)),
                  pl.BlockSpec(wq.shape, lambda i, h: (0, 0)),
                  pl.BlockSpec((1, 2, P_NKEYS, LANES), lambda i, h: (h, 0, 0, 0))],
        out_specs=[ospec, ospec, ospec, ospec],
        out_shape=[outb, out, out, outb],
        scratch_shapes=[pltpu.VMEM((2 * P_HEADS, tm, LANES), F32),
                        pltpu.VMEM((2, P_NKEYS, tm), F32),
                        pltpu.VMEM((2, P_TOPK, tm), F32),
                        pltpu.VMEM((P_TOPK, tm), F32)],
        compiler_params=_params("parallel", "arbitrary"),
        name="peer_select",
    )(xb, wq, sk)


def _peer_gated_act(act_t, blk, rank2_ref, cnt1_ref, e1_ref, e2_ref, nsub):
    parts = []
    for ii in range(nsub):
        i = blk * nsub + ii
        w = None
        for h in range(P_HEADS):
            cnt = cnt1_ref[h, pl.ds(i, 1), :].astype(BF16)
            e1 = e1_ref[h, pl.ds(i, 1), :].astype(BF16)
            t = jnp.where(rank2_ref[h] < cnt, e2_ref[h], jnp.zeros((), BF16)) * e1
            w = t if w is None else w + t
        a = act_t[ii * P_NKEYS:(ii + 1) * P_NKEYS]
        gelu = 0.5 * a * (1.0 + lax.erf(a * (0.5 ** 0.5)))
        parts.append(w * gelu.astype(BF16))
    return jnp.concatenate(parts, axis=0) if nsub > 1 else parts[0]


def _peer_dense_kernel(xb_ref, rank2_ref, cnt1_ref, e1_ref, e2_ref, u_ref, *rest, nsub, pairs):
    vt_refs, (o_ref, pa_ref, pb_ref) = rest[:2 * pairs], rest[2 * pairs:]
    s = pl.program_id(1)
    ns = pl.num_programs(1) - 1
    sub = nsub * P_NKEYS
    sel = (rank2_ref, cnt1_ref, e1_ref, e2_ref)

    @pl.when(s == 0)
    def _():
        o_ref[0] = jnp.zeros(o_ref.shape[1:], F32)
        pa_ref[...] = jnp.zeros_like(pa_ref)

    @pl.when(s < ns)
    def _():
        xb = xb_ref[...]
        for r in range(pairs):
            blk = 2 * (pairs * s + r)
            act0 = _nt_dot(u_ref[2 * r * sub:(2 * r + 1) * sub, :], xb)
            act1 = _nt_dot(u_ref[(2 * r + 1) * sub:(2 * r + 2) * sub, :], xb)
            o_ref[0] += jnp.dot(vt_refs[2 * r][0], pa_ref[...], preferred_element_type=F32)
            pb_ref[...] = _peer_gated_act(act0, blk, *sel, nsub)
            o_ref[0] += jnp.dot(vt_refs[2 * r + 1][0], pb_ref[...], preferred_element_type=F32)
            pa_ref[...] = _peer_gated_act(act1, blk + 1, *sel, nsub)

    @pl.when(s == ns)
    def _():
        o_ref[0] += jnp.dot(vt_refs[0][0], pa_ref[...], preferred_element_type=F32)


def _peer_dense(xb, rank2, cnt1, e1, e2, u_b, vt_b, tm, pairs=2):
    T, D = xb.shape
    tm = min(tm, T)
    nblk, _, be = vt_b.shape
    nsub = be // P_NKEYS
    per_step = 2 * pairs
    ns = nblk // per_step
    sel = pl.BlockSpec((P_HEADS, P_NKEYS, tm), lambda i, s: (0, 0, i))

    def vt_spec(k):
        return pl.BlockSpec((1, D, be),
                            lambda i, s: (jnp.clip(per_step * s - 1 + k, 0, nblk - 1), 0, 0))

    return pl.pallas_call(
        functools.partial(_peer_dense_kernel, nsub=nsub, pairs=pairs),
        grid=(T // tm, ns + 1),
        in_specs=[pl.BlockSpec((tm, D), lambda i, s: (i, 0)), sel, sel, sel, sel,
                  pl.BlockSpec((per_step * be, D), lambda i, s: (jnp.minimum(s, ns - 1), 0))]
                 + [vt_spec(k) for k in range(per_step)],
        out_specs=pl.BlockSpec((1, D, tm), lambda i, s: (i, 0, 0)),
        out_shape=jax.ShapeDtypeStruct((T // tm, D, tm), F32),
        scratch_shapes=[pltpu.VMEM((be, tm), BF16), pltpu.VMEM((be, tm), BF16)],
        compiler_params=_params("parallel", "arbitrary"),
        name="peer_dense",
    )(xb, rank2, cnt1, e1, e2, u_b, *([vt_b] * per_step))


def _pad_row(vals_at, width=LANES):
    row = jnp.zeros((1, width), F32)
    for off, v in vals_at:
        row = row.at[0, off:off + v.shape[0]].set(v.astype(F32))
    return row


def _rope_tables(S):
    half = A_HEADDIM // 2
    freqs = ROPE_THETA ** (-jnp.arange(half, dtype=F32) / half)
    ang = jnp.arange(S, dtype=F32)[:, None] * freqs[None, :]
    reps = LANES // half
    return jnp.tile(jnp.cos(ang), (1, reps)), jnp.tile(jnp.sin(ang), (1, reps))


def _head_expand_matrix():
    e = np.zeros((S_GROUPS, LANES, S_HPG * S_HEADDIM), np.float32)
    for g in range(S_GROUPS):
        for hh in range(S_HPG):
            e[g, SM_DT + g * S_HPG + hh, hh * S_HEADDIM:(hh + 1) * S_HEADDIM] = 1.0
    return jnp.asarray(e)


def _layer(x, xb, p, consts, B, S, alpha):
    T, D = x.shape
    w_in = p["w_in"]
    o_m = 2 * M_HEADS * M_DQK + 2 * M_HEADS * M_DV
    o_s = o_m + 2 * M_HEADS
    o_dt = o_s + S_DINNER + S_CONV_CH
    o_a = o_dt + S_HEADS
    o_g = o_a + (A_QHEADS + 2 * A_KVHEADS) * A_HEADDIM
    w_m = w_in[:, :o_m].astype(BF16)
    w_s = w_in[:, o_s:o_dt].astype(BF16)
    w_a = w_in[:, o_a:o_g].astype(BF16)
    w_g = w_in[:, o_g:].astype(BF16)
    w_sm = jnp.concatenate([w_in[:, o_m:o_s], w_in[:, o_dt:o_a],
                            jnp.zeros((D, LANES - 2 * M_HEADS - S_HEADS), F32)], axis=1).astype(BF16)

    mp = _matmul(xb, w_m, 1024, 1536)
    sp = _matmul(xb, w_s, 1024, 1280)
    ap = _matmul(xb, w_a, 1024, 1536)
    sm = _matmul(xb, w_sm, 1024, LANES)

    brow = _pad_row([(SM_I, p["mlstm_gate_b"][0]), (SM_F, p["mlstm_gate_b"][1]),
                     (SM_DT, p["ssm_dt_bias"])])
    arow = _pad_row([(SM_DT, p["ssm_a_log"])])
    drow = jnp.repeat(p["ssm_d"].astype(F32), S_HEADDIM)[None, :]
    y_m = _mlstm(mp, sm, brow, p["mlstm_norm_w"][None, :], B, S)
    y_s = _ssd(sp, sm, brow, arow, drow, p["ssm_conv_w"][:, 0, :], p["ssm_conv_b"][None, :],
               p["ssm_norm_w"][None, :], consts["e_mat"], B, S)
    y_a = _swa(ap, consts["cos"], consts["sin"], _pad_row([(0, p["swa_sinks"])]), B, S)

    mix = _merge(xb, y_m, y_s, y_a, w_g, p["merge_gate_b"].reshape(1, 3 * D),
                 p["w_branch"].astype(BF16), 1024, 512)
    x1, x1b = _outproj(x, mix, p["w_out"].astype(BF16), p["ln1_g"][None, :], p["ln1_b"][None, :],
                       alpha, 512)

    rank2, cnt1, e1, e2 = _peer_select(x1b, p["peer_wq"].astype(BF16), p["peer_subkeys"], 512)
    be = 256
    vt_blocks = jnp.swapaxes(p["peer_v"].astype(BF16).reshape(P_EXPERTS // be, be, D), 1, 2)
    peer = _peer_dense(x1b, rank2, cnt1, e1, e2, p["peer_u"].astype(BF16), vt_blocks, 512)
    return _res_ln(x1, peer, p["ln2_g"][None, :], p["ln2_b"][None, :], alpha)


def _forward(x, params, depth):
    B, S, D = x.shape
    alpha = (2.0 * depth) ** 0.25
    cos_t, sin_t = _rope_tables(S)
    consts = {"cos": cos_t, "sin": sin_t, "e_mat": _head_expand_matrix()}
    xf = x.reshape(B * S, D)

    def body(carry, p):
        xc, xcb = carry
        return _layer(xc, xcb, p, consts, B, S, alpha), None

    (xf, _), _ = lax.scan(body, (xf, xf.astype(BF16)), params)
    return xf.reshape(B, S, D)


def kernel(x, w_in, mlstm_gate_b, mlstm_norm_w, ssm_conv_w, ssm_conv_b, ssm_dt_bias, ssm_a_log,
           ssm_d, ssm_norm_w, swa_sinks, merge_gate_b, w_branch, w_out, ln1_g, ln1_b,
           peer_wq, peer_subkeys, peer_u, peer_v, ln2_g, ln2_b):
    params = dict(w_in=w_in, mlstm_gate_b=mlstm_gate_b, mlstm_norm_w=mlstm_norm_w,
                  ssm_conv_w=ssm_conv_w, ssm_conv_b=ssm_conv_b, ssm_dt_bias=ssm_dt_bias,
                  ssm_a_log=ssm_a_log, ssm_d=ssm_d, ssm_norm_w=ssm_norm_w, swa_sinks=swa_sinks,
                  merge_gate_b=merge_gate_b, w_branch=w_branch, w_out=w_out, ln1_g=ln1_g,
                  ln1_b=ln1_b, peer_wq=peer_wq, peer_subkeys=peer_subkeys, peer_u=peer_u,
                  peer_v=peer_v, ln2_g=ln2_g, ln2_b=ln2_b)
    return _forward(x, params, w_in.shape[0])
```
